```python
import math
import jax, jax.numpy as jnp
from jax import lax
import numpy as np

D_MODEL = 1024
BATCH = 8
SEQ = 4096
DEPTH = 1

HEAD_DIM = 128
N_HEADS_MOBA = 8
N_HEADS_GDN = 8
MOBA_WIDTH = N_HEADS_MOBA * HEAD_DIM
GDN_WIDTH = N_HEADS_GDN * HEAD_DIM
MOBA_BLOCK = 256
MOBA_TOPK = 3
MOBA_Q_CHUNK = 32
GDN_CHUNK = 64
CONV_WIDTH = 4
ROPE_THETA = 10000.0
D_FF = -(-8 * D_MODEL // (3 * 256)) * 256
PLE_DIM = 256
EPS = 1e-6

IN_SIZES = (MOBA_WIDTH, MOBA_WIDTH, MOBA_WIDTH, 3 * GDN_WIDTH, GDN_WIDTH,
            N_HEADS_GDN, N_HEADS_GDN, D_MODEL, D_MODEL)
IN_SPLITS = tuple(int(s) for s in np.cumsum(IN_SIZES)[:-1])
IN_WIDTH = int(sum(IN_SIZES))

kernel_name = "hybrid_moba_gated_deltanet_block"


def rms_norm(x, w):
    xf = x.astype(jnp.float32)
    y = xf * lax.rsqrt(jnp.mean(xf * xf, axis=-1, keepdims=True) + EPS)
    return (y * w.astype(jnp.float32)).astype(x.dtype)


def l2_norm(x):
    xf = x.astype(jnp.float32)
    return xf * lax.rsqrt(jnp.sum(xf * xf, axis=-1, keepdims=True) + EPS)


def rope_tables(seq):
    inv = 1.0 / (ROPE_THETA ** (jnp.arange(0, HEAD_DIM, 2, dtype=jnp.float32) / HEAD_DIM))
    ang = jnp.arange(seq, dtype=jnp.float32)[:, None] * inv[None, :]
    return jnp.cos(ang), jnp.sin(ang)


def apply_rope(x, cos, sin):
    x1, x2 = jnp.split(x.astype(jnp.float32), 2, axis=-1)
    out = jnp.concatenate([x1 * cos - x2 * sin, x2 * cos + x1 * sin], axis=-1)
    return out.astype(x.dtype)


def causal_depthwise_conv(x, w):
    c = x.shape[-1]
    return lax.conv_general_dilated(
        x, w[:, None, :].astype(x.dtype), window_strides=(1,),
        padding=[(CONV_WIDTH - 1, 0)], dimension_numbers=("NWC", "WIO", "NWC"),
        feature_group_count=c)


def moba_attention(q, k, v):
    b, h, s, hd = q.shape
    nb = -(-s // MOBA_BLOCK)
    pad = nb * MOBA_BLOCK - s
    n_sel = min(MOBA_TOPK, nb)
    scale = hd ** -0.5
    k_blocks = jnp.pad(k, ((0, 0), (0, 0), (0, pad), (0, 0))).reshape(b, h, nb, MOBA_BLOCK, hd)
    v_blocks = jnp.pad(v, ((0, 0), (0, 0), (0, pad), (0, 0))).reshape(b, h, nb, MOBA_BLOCK, hd)
    k_mean = jnp.mean(k_blocks.astype(jnp.float32), axis=3)
    blk_ids = jnp.arange(nb)
    n_chunks = s // MOBA_Q_CHUNK

    def one_chunk(c):
        start = c * MOBA_Q_CHUNK
        q_c = lax.dynamic_slice_in_dim(q, start, MOBA_Q_CHUNK, axis=2)
        q_blk = start // MOBA_BLOCK
        gate = jnp.einsum("bhcd,bhnd->bhcn", q_c.astype(jnp.float32), k_mean)
        gate = jnp.where(blk_ids < q_blk, gate, -jnp.inf)
        _, sel = lax.top_k(gate, n_sel)
        sel_valid = sel < q_blk
        idx = sel.reshape(b, h, MOBA_Q_CHUNK * n_sel, 1, 1)
        k_sel = jnp.take_along_axis(k_blocks, idx, axis=2).reshape(
            b, h, MOBA_Q_CHUNK, n_sel, MOBA_BLOCK, hd)
        v_sel = jnp.take_along_axis(v_blocks, idx, axis=2).reshape(
            b, h, MOBA_Q_CHUNK, n_sel, MOBA_BLOCK, hd)
        s_past = jnp.einsum("bhcd,bhcknd->bhckn", q_c, k_sel).astype(jnp.float32) * scale
        s_past = jnp.where(sel_valid[..., None], s_past, -jnp.inf)
        s_past = s_past.reshape(b, h, MOBA_Q_CHUNK, n_sel * MOBA_BLOCK)
        k_own = lax.dynamic_index_in_dim(k_blocks, q_blk, axis=2, keepdims=False)
        v_own = lax.dynamic_index_in_dim(v_blocks, q_blk, axis=2, keepdims=False)
        s_own = jnp.einsum("bhcd,bhnd->bhcn", q_c, k_own).astype(jnp.float32) * scale
        q_pos = start + jnp.arange(MOBA_Q_CHUNK)
        k_pos = q_blk * MOBA_BLOCK + jnp.arange(MOBA_BLOCK)
        s_own = jnp.where(k_pos[None, :] <= q_pos[:, None], s_own, -jnp.inf)
        probs = jax.nn.softmax(jnp.concatenate([s_past, s_own], axis=-1), axis=-1)
        p_past = probs[..., : n_sel * MOBA_BLOCK].reshape(
            b, h, MOBA_Q_CHUNK, n_sel, MOBA_BLOCK).astype(v.dtype)
        p_own = probs[..., n_sel * MOBA_BLOCK:].astype(v.dtype)
        return (jnp.einsum("bhckn,bhcknd->bhcd", p_past, v_sel)
                + jnp.einsum("bhcn,bhnd->bhcd", p_own, v_own))

    out = lax.map(one_chunk, jnp.arange(n_chunks))
    return out.transpose(1, 2, 0, 3, 4).reshape(b, h, s, hd)


def gated_delta_rule(q, k, v, g, beta):
    b, h, s, dk = q.shape
    dv = v.shape[-1]
    L = GDN_CHUNK
    nc = s // L
    f32 = jnp.float32
    q = q.astype(f32) * (dk ** -0.5)
    k = k.astype(f32)
    v = v.astype(f32)
    q = q.reshape(b, h, nc, L, dk)
    k = k.reshape(b, h, nc, L, dk)
    v = v.reshape(b, h, nc, L, dv)
    g = g.astype(f32).reshape(b, h, nc, L)
    beta = beta.astype(f32).reshape(b, h, nc, L)
    gc = jnp.cumsum(g, axis=-1)
    causal = jnp.tril(jnp.ones((L, L), dtype=bool))
    strict = jnp.tril(jnp.ones((L, L), dtype=bool), k=-1)
    decay = jnp.exp(jnp.where(causal, gc[..., :, None] - gc[..., None, :], -jnp.inf))
    k_beta = k * beta[..., None]
    a_low = jnp.where(strict, jnp.einsum("bhnid,bhnjd->bhnij", k_beta, k) * decay, 0.0)
    t_mat = a_low + jnp.eye(L, dtype=f32)
    rhs = jnp.concatenate([v * beta[..., None], k_beta * jnp.exp(gc)[..., None]], axis=-1)
    sol = lax.linalg.triangular_solve(t_mat, rhs, left_side=True, lower=True, unit_diagonal=True)
    u = sol[..., :dv]
    w = sol[..., dv:]
    attn = jnp.where(causal, jnp.einsum("bhnid,bhnjd->bhnij", q, k) * decay, 0.0)

    def step(state, inp):
        q_i, k_i, u_i, w_i, g_i, a_i = inp
        v_new = u_i - jnp.einsum("bhld,bhde->bhle", w_i, state)
        o = (jnp.einsum("bhld,bhde->bhle", q_i * jnp.exp(g_i)[..., None], state)
             + jnp.einsum("bhlm,bhme->bhle", a_i, v_new))
        g_last = g_i[..., -1]
        state = (state * jnp.exp(g_last)[..., None, None]
                 + jnp.einsum("bhld,bhle->bhde", k_i * jnp.exp(g_last[..., None] - g_i)[..., None], v_new))
        return state, o

    xs = tuple(jnp.moveaxis(t, 2, 0) for t in (q, k, u, w, gc, attn))
    state0 = jnp.zeros((b, h, dk, dv), dtype=f32)
    _, o = lax.scan(step, state0, xs)
    return jnp.moveaxis(o, 0, 2).reshape(b, h, s, dv)


def setup_inputs(seed: int = 0) -> dict:
    key = jax.random.key(seed)
    ks = jax.random.split(key, 24)
    f32 = jnp.float32

    def nrm(k, shape, scale):
        return jax.random.normal(k, shape, f32) * scale

    def gain(k, n):
        return 1.0 + 0.1 * jax.random.normal(k, (DEPTH, n), f32)

    dt = jnp.exp(jax.random.uniform(ks[7], (DEPTH, N_HEADS_GDN), f32,
                                    math.log(1e-3), math.log(1e-1)))
    return {
        "x": nrm(ks[0], (BATCH, SEQ, D_MODEL), 1.0),
        "p": nrm(ks[1], (DEPTH, BATCH, SEQ, PLE_DIM), 1.0),
        "ln_pre_mix": gain(ks[2], D_MODEL),
        "w_in": nrm(ks[3], (DEPTH, D_MODEL, IN_WIDTH), D_MODEL ** -0.5),
        "conv_w": nrm(ks[4], (DEPTH, CONV_WIDTH, 3 * GDN_WIDTH), CONV_WIDTH ** -0.5),
        "a_log": jnp.log(jax.random.uniform(ks[5], (DEPTH, N_HEADS_GDN), f32, 1.0, 16.0)),
        "dt_bias": jnp.log(jnp.expm1(dt)),
        "gdn_norm": gain(ks[6], HEAD_DIM),
        "w_proj_a": nrm(ks[8], (DEPTH, MOBA_WIDTH, D_MODEL), MOBA_WIDTH ** -0.5),
        "w_proj_b": nrm(ks[9], (DEPTH, GDN_WIDTH, D_MODEL), GDN_WIDTH ** -0.5),
        "w_out": nrm(ks[10], (DEPTH, D_MODEL, D_MODEL), D_MODEL ** -0.5),
        "ln_post_mix": gain(ks[11], D_MODEL),
        "ln_pre_ffn": gain(ks[12], D_MODEL),
        "w_ffn_gate": nrm(ks[13], (DEPTH, D_MODEL, D_FF), D_MODEL ** -0.5),
        "w_ffn_up": nrm(ks[14], (DEPTH, D_MODEL, D_FF), D_MODEL ** -0.5),
        "w_ffn_down": nrm(ks[15], (DEPTH, D_FF, D_MODEL), D_FF ** -0.5),
        "ln_post_ffn": gain(ks[16], D_MODEL),
        "w_ple": nrm(ks[17], (DEPTH, PLE_DIM, D_MODEL), PLE_DIM ** -0.5),
        "ln_ple": gain(ks[18], D_MODEL),
        "w_ple_gate": nrm(ks[19], (DEPTH, D_MODEL, D_MODEL), D_MODEL ** -0.5),
    }


def reference(x, p, ln_pre_mix, w_in, conv_w, a_log, dt_bias, gdn_norm, w_proj_a, w_proj_b,
              w_out, ln_post_mix, ln_pre_ffn, w_ffn_gate, w_ffn_up, w_ffn_down, ln_post_ffn,
              w_ple, ln_ple, w_ple_gate):
    b, s, _ = x.shape
    cos, sin = rope_tables(s)

    def to_heads(t, n):
        return t.reshape(b, s, n, HEAD_DIM).transpose(0, 2, 1, 3)

    def from_heads(t):
        return t.transpose(0, 2, 1, 3).reshape(b, s, -1)

    h = x
    for i in range(DEPTH):
        u = rms_norm(h, ln_pre_mix[i])
        proj = u @ w_in[i]
        q_a, k_a, v_a, qkv_b, z_b, b_b, a_b, gate_a, gate_b = jnp.split(proj, IN_SPLITS, axis=-1)

        qa = apply_rope(to_heads(q_a, N_HEADS_MOBA), cos, sin)
        ka = apply_rope(to_heads(k_a, N_HEADS_MOBA), cos, sin)
        va = to_heads(v_a, N_HEADS_MOBA)
        y_a = from_heads(moba_attention(qa, ka, va))

        qkv_b = jax.nn.silu(causal_depthwise_conv(qkv_b, conv_w[i]))
        q_b, k_b, v_b = jnp.split(qkv_b, 3, axis=-1)
        qb = l2_norm(to_heads(q_b, N_HEADS_GDN))
        kb = l2_norm(to_heads(k_b, N_HEADS_GDN))
        vb = to_heads(v_b, N_HEADS_GDN)
        g_log = -jnp.exp(a_log[i].astype(jnp.float32)) * jax.nn.softplus(
            a_b.astype(jnp.float32) + dt_bias[i].astype(jnp.float32))
        beta = jax.nn.sigmoid(b_b.astype(jnp.float32))
        o_b = gated_delta_rule(qb, kb, vb, g_log.transpose(0, 2, 1), beta.transpose(0, 2, 1))
        o_b = o_b.transpose(0, 2, 1, 3).astype(h.dtype)
        z = z_b.reshape(b, s, N_HEADS_GDN, HEAD_DIM)
        y_b = (rms_norm(o_b, gdn_norm[i]) * jax.nn.silu(z)).reshape(b, s, GDN_WIDTH)

        merged = (jax.nn.sigmoid(gate_a) * (y_a @ w_proj_a[i])
                  + jax.nn.sigmoid(gate_b) * (y_b @ w_proj_b[i]))
        h = h + rms_norm(merged @ w_out[i], ln_post_mix[i])

        f = rms_norm(h, ln_pre_ffn[i])
        f = (jax.nn.silu(f @ w_ffn_gate[i]) * (f @ w_ffn_up[i])) @ w_ffn_down[i]
        h = h + rms_norm(f, ln_post_ffn[i])

        e = rms_norm(p[i].astype(h.dtype) @ w_ple[i], ln_ple[i])
        h = h + jax.nn.sigmoid(h @ w_ple_gate[i]) * e
    return h
```

```python
import functools

import jax
import jax.numpy as jnp
from jax import lax
from jax.experimental import pallas as pl
from jax.experimental.pallas import tpu as pltpu

F32 = jnp.float32
BF16 = jnp.bfloat16

HEAD_DIM = 128
N_HEADS = 8
MOBA_BLOCK = 256
MOBA_TOPK = 3
GDN_CHUNK = 64
GDN_ROWS = 256
CONV_WIDTH = 4
EPS = 1e-6
ROPE_THETA = 10000.0
NEG_BIG = -1e30
VMEM_LIMIT = 56 * 1024 * 1024

_NT = (((1,), (1,)), ((), ()))
_TN = (((0,), (0,)), ((), ()))


def _dot(a, b):
    return jnp.dot(a, b, preferred_element_type=F32)


def _dot_nt(a, b):
    return lax.dot_general(a, b, _NT, preferred_element_type=F32)


def _dot_tn(a, b):
    return lax.dot_general(a, b, _TN, preferred_element_type=F32)


def _rms(x, w):
    return x * lax.rsqrt(jnp.mean(x * x, axis=-1, keepdims=True) + EPS) * w


def _sigmoid(x):
    return 1.0 / (1.0 + jnp.exp(-x))


def _silu(x):
    return x * _sigmoid(x)


def _softplus(x):
    return jnp.maximum(x, 0.0) + jnp.log1p(jnp.exp(-jnp.abs(x)))


def _inproj_kernel(x_ref, ln_ref, w_ref, ws_ref, wst_ref, cos_ref, sin_ref,
                   o_ref, ab_ref, abt_ref, u_ref):
    j = pl.program_id(1)

    @pl.when(j == 0)
    def _():
        u = _rms(x_ref[...], ln_ref[...]).astype(BF16)
        u_ref[...] = u
        ab_ref[...] = _dot(u, ws_ref[...])
        abt_ref[...] = _dot_nt(wst_ref[...], u)

    acc = _dot(u_ref[...], w_ref[...])

    @pl.when(j < 2)
    def _():
        cosf = cos_ref[...]
        sinf = sin_ref[...]
        qscale = jnp.where(j == 0, HEAD_DIM ** -0.5, 1.0).astype(F32)
        for h in range(N_HEADS):
            seg = acc[:, h * HEAD_DIM:(h + 1) * HEAD_DIM]
            rot = seg * cosf + pltpu.roll(seg, HEAD_DIM // 2, axis=1) * sinf
            o_ref[:, h * HEAD_DIM:(h + 1) * HEAD_DIM] = (rot * qscale).astype(BF16)

    @pl.when(j >= 2)
    def _():
        o_ref[...] = acc.astype(BF16)


def _inproj(x2, ln, w_main, w_small, w_small_t, cosf, sinf, seq, tm):
    t, d = x2.shape
    n_main = w_main.shape[1]
    tn = N_HEADS * HEAD_DIM
    n_seq_tiles = seq // tm
    return pl.pallas_call(
        _inproj_kernel,
        grid=(t // tm, n_main // tn),
        in_specs=[
            pl.BlockSpec((tm, d), lambda i, j: (i, 0)),
            pl.BlockSpec((1, d), lambda i, j: (0, 0)),
            pl.BlockSpec((d, tn), lambda i, j: (0, j)),
            pl.BlockSpec((d, HEAD_DIM), lambda i, j: (0, 0)),
            pl.BlockSpec((HEAD_DIM, d), lambda i, j: (0, 0)),
            pl.BlockSpec((tm, HEAD_DIM), lambda i, j: (i % n_seq_tiles, 0)),
            pl.BlockSpec((tm, HEAD_DIM), lambda i, j: (i % n_seq_tiles, 0)),
        ],
        out_specs=[
            pl.BlockSpec((tm, tn), lambda i, j: (i, j)),
            pl.BlockSpec((tm, HEAD_DIM), lambda i, j: (i, 0)),
            pl.BlockSpec((HEAD_DIM, tm), lambda i, j: (0, i)),
        ],
        out_shape=[
            jax.ShapeDtypeStruct((t, n_main), BF16),
            jax.ShapeDtypeStruct((t, HEAD_DIM), F32),
            jax.ShapeDtypeStruct((HEAD_DIM, t), F32),
        ],
        scratch_shapes=[pltpu.VMEM((tm, d), BF16)],
        compiler_params=pltpu.CompilerParams(
            dimension_semantics=("arbitrary", "arbitrary"), vmem_limit_bytes=VMEM_LIMIT),
        name="inproj",
    )(x2, ln, w_main, w_small, w_small_t, cosf, sinf)


def _moba_kernel(q_ref, k_ref, v_ref, o_ref, kmh_ref, kml_ref):
    i = pl.program_id(2)
    blk = MOBA_BLOCK
    n_blocks = k_ref.shape[0] // blk

    @pl.when(i == 0)
    def _():
        kmh_ref[...] = jnp.zeros_like(kmh_ref)
        kml_ref[...] = jnp.zeros_like(kml_ref)
        km = jnp.mean(k_ref[...].astype(F32).reshape(n_blocks, blk, HEAD_DIM), axis=1)
        hi = km.astype(BF16)
        kmh_ref[0:n_blocks, :] = hi
        kml_ref[0:n_blocks, :] = (km - hi.astype(F32)).astype(BF16)

    q = q_ref[...]
    lane = lax.broadcasted_iota(jnp.int32, (blk, HEAD_DIM), 1)

    gate = _dot_nt(q, kmh_ref[...]) + _dot_nt(q, kml_ref[...])
    g = jnp.where(lane < i, gate, -jnp.inf)
    lane_f = lane.astype(F32)
    sel = jnp.zeros((blk, HEAD_DIM), jnp.bool_)
    for _ in range(MOBA_TOPK):
        mx = jnp.max(g, axis=1, keepdims=True)
        idx = jnp.min(jnp.where(g == mx, lane_f, float(HEAD_DIM)), axis=1, keepdims=True)
        hit = lane_f == idx
        sel = jnp.logical_or(sel, hit)
        g = jnp.where(hit, -jnp.inf, g)
    sel = jnp.logical_and(sel, lane < i)
    q_aug = jnp.concatenate([q, jnp.where(sel, 0.0, NEG_BIG).astype(BF16)], axis=1)

    k0 = k_ref[pl.ds(pl.multiple_of(i * blk, blk), blk), :]
    v0 = v_ref[pl.ds(pl.multiple_of(i * blk, blk), blk), :]
    s = _dot_nt(q, k0)
    row = lax.broadcasted_iota(jnp.int32, (blk, blk), 0)
    col = lax.broadcasted_iota(jnp.int32, (blk, blk), 1)
    s = jnp.where(col <= row, s, -jnp.inf)
    m0 = jnp.max(s, axis=1, keepdims=True)
    p = jnp.exp(s - m0)
    l0 = jnp.sum(p, axis=1, keepdims=True)
    acc0 = _dot(p.astype(BF16), v0)

    def body(jb, carry):
        m_i, l_i, acc = carry
        off = pl.multiple_of(jb * blk, blk)
        kj = k_ref[pl.ds(off, blk), :]
        vj = v_ref[pl.ds(off, blk), :]
        k_aug = jnp.concatenate([kj, jnp.where(lane == jb, 1.0, 0.0).astype(BF16)], axis=1)
        sj = _dot_nt(q_aug, k_aug)
        m_new = jnp.maximum(m_i, jnp.max(sj, axis=1, keepdims=True))
        alpha = jnp.exp(m_i - m_new)
        pj = jnp.exp(sj - m_new)
        l_new = alpha * l_i + jnp.sum(pj, axis=1, keepdims=True)
        acc_new = alpha * acc + _dot(pj.astype(BF16), vj)
        return m_new, l_new, acc_new

    _, l_f, acc_f = lax.fori_loop(0, i, body, (m0, l0, acc0))
    o_ref[...] = (acc_f / l_f).astype(BF16)


def _moba(proj, batch, seq):
    t = proj.shape[0]
    nb = seq // MOBA_BLOCK
    return pl.pallas_call(
        _moba_kernel,
        grid=(batch, N_HEADS, nb),
        in_specs=[
            pl.BlockSpec((MOBA_BLOCK, HEAD_DIM), lambda b, h, i: (b * nb + i, h)),
            pl.BlockSpec((seq, HEAD_DIM), lambda b, h, i: (b, N_HEADS + h)),
            pl.BlockSpec((seq, HEAD_DIM), lambda b, h, i: (b, 2 * N_HEADS + h)),
        ],
        out_specs=pl.BlockSpec((MOBA_BLOCK, HEAD_DIM), lambda b, h, i: (b * nb + i, h)),
        out_shape=jax.ShapeDtypeStruct((t, N_HEADS * HEAD_DIM), BF16),
        scratch_shapes=[pltpu.VMEM((HEAD_DIM, HEAD_DIM), BF16),
                        pltpu.VMEM((HEAD_DIM, HEAD_DIM), BF16)],
        compiler_params=pltpu.CompilerParams(
            dimension_semantics=("arbitrary", "arbitrary", "arbitrary"),
            vmem_limit_bytes=VMEM_LIMIT),
        name="moba",
    )(proj, proj, proj)


def _chunk_masks(n):
    row = lax.broadcasted_iota(jnp.int32, (n, n), 0)
    col = lax.broadcasted_iota(jnp.int32, (n, n), 1)
    shift = GDN_CHUNK.bit_length() - 1
    same = (row >> shift) == (col >> shift)
    return row, col, same


def _gdn_gates_kernel(ab_ref, abt_ref, alog_row_ref, dt_row_ref, alog_col_ref, dt_col_ref,
                      col_ref, rowf_ref):
    r = ab_ref.shape[0]
    hi = lax.Precision.HIGHEST
    row, col, same = _chunk_masks(r)
    tril = jnp.where(jnp.logical_and(same, col <= row), 1.0, 0.0).astype(F32)
    ones = jnp.where(same, 1.0, 0.0).astype(F32)

    ab = ab_ref[...]
    lane = lax.broadcasted_iota(jnp.int32, ab.shape, 1)
    is_b = lane < N_HEADS
    is_a = jnp.logical_and(lane >= N_HEADS, lane < 2 * N_HEADS)
    g = -jnp.exp(alog_row_ref[...]) * _softplus(ab + dt_row_ref[...])
    g = jnp.where(is_a, g, 0.0)
    gc = jnp.dot(tril, g, precision=hi, preferred_element_type=F32)
    glast = jnp.dot(ones, g, precision=hi, preferred_element_type=F32)
    out = jnp.where(is_b, _sigmoid(ab), gc) + pltpu.roll(glast, N_HEADS, axis=1)
    col_ref[...] = out

    abt = abt_ref[...]
    srow = lax.broadcasted_iota(jnp.int32, abt.shape, 0)
    gt = -jnp.exp(alog_col_ref[...]) * _softplus(abt + dt_col_ref[...])
    gt = jnp.where(jnp.logical_and(srow >= N_HEADS, srow < 2 * N_HEADS), gt, 0.0)
    triu = jnp.where(jnp.logical_and(same, row <= col), 1.0, 0.0).astype(F32)
    rowf_ref[...] = jnp.dot(gt, triu, precision=hi, preferred_element_type=F32)


def _gdn_gates(ab, abt, alog_row, dt_row, alog_col, dt_col):
    t = ab.shape[0]
    r = GDN_ROWS
    return pl.pallas_call(
        _gdn_gates_kernel,
        grid=(t // r,),
        in_specs=[
            pl.BlockSpec((r, HEAD_DIM), lambda i: (i, 0)),
            pl.BlockSpec((HEAD_DIM, r), lambda i: (0, i)),
            pl.BlockSpec((1, HEAD_DIM), lambda i: (0, 0)),
            pl.BlockSpec((1, HEAD_DIM), lambda i: (0, 0)),
            pl.BlockSpec((HEAD_DIM, 1), lambda i: (0, 0)),
            pl.BlockSpec((HEAD_DIM, 1), lambda i: (0, 0)),
        ],
        out_specs=[
            pl.BlockSpec((r, HEAD_DIM), lambda i: (i, 0)),
            pl.BlockSpec((HEAD_DIM, r), lambda i: (0, i)),
        ],
        out_shape=[
            jax.ShapeDtypeStruct((t, HEAD_DIM), F32),
            jax.ShapeDtypeStruct((HEAD_DIM, t), F32),
        ],
        compiler_params=pltpu.CompilerParams(dimension_semantics=("arbitrary",)),
        name="gdn_gates",
    )(ab, abt, alog_row, dt_row, alog_col, dt_col)


def _gdn_kernel(q_ref, k_ref, v_ref, z_ref, colf_ref, rowf_ref, cw_ref, gn_ref,
                o_ref, xe_ref, state_ref):
    h = pl.program_id(1)
    sblk = pl.program_id(2)
    r = q_ref.shape[0]
    n_chunks = r // GDN_CHUNK
    pad = 8

    @pl.when(sblk == 0)
    def _():
        state_ref[...] = jnp.zeros_like(state_ref)
        xe_ref[:, 0:pad, :] = jnp.zeros((3, pad, HEAD_DIM), F32)

    def conv_silu(c, raw_ref):
        x = raw_ref[...].astype(F32)
        xe_ref[c, pad:pad + r, :] = x
        w = cw_ref[c]
        y = x * w[CONV_WIDTH - 1:CONV_WIDTH, :]
        for d in range(1, CONV_WIDTH):
            y = y + xe_ref[c, pad - d:pad - d + r, :] * w[CONV_WIDTH - 1 - d:CONV_WIDTH - d, :]
        xe_ref[c, 0:pad, :] = x[r - pad:r, :]
        return _silu(y)

    q = conv_silu(0, q_ref)
    k = conv_silu(1, k_ref)
    v = conv_silu(2, v_ref)
    q = q * lax.rsqrt(jnp.sum(q * q, axis=-1, keepdims=True) + EPS) * (HEAD_DIM ** -0.5)
    k = k * lax.rsqrt(jnp.sum(k * k, axis=-1, keepdims=True) + EPS)

    colf = colf_ref[...]
    lane = lax.broadcasted_iota(jnp.int32, colf.shape, 1)

    def pick(lane_id):
        return jnp.sum(jnp.where(lane == lane_id, colf, 0.0), axis=1, keepdims=True)

    beta = pick(h)
    gc = pick(N_HEADS + h)
    glast = pick(2 * N_HEADS + h)
    gc_row = rowf_ref[pl.ds(N_HEADS + h, 1), :]

    row, col, same = _chunk_masks(r)
    causal = jnp.logical_and(same, col <= row)
    strict = jnp.logical_and(same, col < row)
    decay = jnp.exp(jnp.where(causal, gc - gc_row, -jnp.inf))

    kb = k * beta
    k16 = k.astype(BF16)
    a_low = jnp.where(strict, _dot_nt(kb.astype(BF16), k16) * decay, 0.0)
    attn = _dot_nt(q.astype(BF16), k16) * decay

    eye = jnp.where(row == col, 1.0, 0.0).astype(F32)
    t_inv = eye - a_low
    a_pow = a_low
    n_sq = (GDN_CHUNK - 1).bit_length() - 1
    for _ in range(n_sq):
        a16 = a_pow.astype(BF16)
        a_pow = _dot(a16, a16)
        t_inv = t_inv + _dot(t_inv.astype(BF16), a_pow.astype(BF16))

    rhs = jnp.concatenate([v * beta, kb * jnp.exp(gc)], axis=1)
    sol = _dot(t_inv.astype(BF16), rhs.astype(BF16))
    sol16 = sol.astype(BF16)
    aw_au = _dot(attn.astype(BF16), sol16)
    p_mat = q * jnp.exp(gc) - aw_au[:, HEAD_DIM:]
    r_mat = aw_au[:, :HEAD_DIM]
    kd = (k * jnp.exp(glast - gc)).astype(BF16)
    e_chunk = jnp.exp(glast)

    state = state_ref[...]
    outs = []
    for c in range(n_chunks):
        lo, hi = c * GDN_CHUNK, (c + 1) * GDN_CHUNK
        nu_g = _dot_tn(kd[lo:hi, :], sol16[lo:hi, :])
        s16 = state.astype(BF16)
        outs.append(_dot(p_mat[lo:hi, :].astype(BF16), s16) + r_mat[lo:hi, :])
        state = (e_chunk[lo:lo + 1, :] * state
                 - _dot(nu_g[:, HEAD_DIM:].astype(BF16), s16) + nu_g[:, :HEAD_DIM])
    state_ref[...] = state
    o = jnp.concatenate(outs, axis=0)

    z = z_ref[...].astype(F32)
    o_ref[...] = (_rms(o, gn_ref[...]) * _silu(z)).astype(BF16)


def _gdn(proj, colf, rowf, conv_w3, gdn_norm, batch, seq):
    t = proj.shape[0]
    r = GDN_ROWS
    ns = seq // r
    qkv0 = 3 * N_HEADS
    z0 = 6 * N_HEADS

    def tok(off):
        return pl.BlockSpec((r, HEAD_DIM), lambda b, h, s: (b * ns + s, off + h))

    return pl.pallas_call(
        _gdn_kernel,
        grid=(batch, N_HEADS, ns),
        in_specs=[
            tok(qkv0), tok(qkv0 + N_HEADS), tok(qkv0 + 2 * N_HEADS), tok(z0),
            pl.BlockSpec((r, HEAD_DIM), lambda b, h, s: (b * ns + s, 0)),
            pl.BlockSpec((HEAD_DIM, r), lambda b, h, s: (0, b * ns + s)),
            pl.BlockSpec((3, CONV_WIDTH, HEAD_DIM), lambda b, h, s: (0, 0, h)),
            pl.BlockSpec((1, HEAD_DIM), lambda b, h, s: (0, 0)),
        ],
        out_specs=pl.BlockSpec((r, HEAD_DIM), lambda b, h, s: (b * ns + s, h)),
        out_shape=jax.ShapeDtypeStruct((t, N_HEADS * HEAD_DIM), BF16),
        scratch_shapes=[pltpu.VMEM((3, r + 8, HEAD_DIM), F32),
                        pltpu.VMEM((HEAD_DIM, HEAD_DIM), F32)],
        compiler_params=pltpu.CompilerParams(
            dimension_semantics=("arbitrary", "arbitrary", "arbitrary"),
            vmem_limit_bytes=VMEM_LIMIT),
        name="gdn",
    )(proj, proj, proj, proj, colf, rowf, conv_w3, gdn_norm)


def _merge_kernel(ya_ref, yb_ref, ga_ref, gb_ref, x_ref, wa_ref, wb_ref, wo_ref, ln_ref, o_ref):
    merged = (_sigmoid(ga_ref[...].astype(F32)) * _dot(ya_ref[...], wa_ref[...])
              + _sigmoid(gb_ref[...].astype(F32)) * _dot(yb_ref[...], wb_ref[...]))
    out = _dot(merged.astype(BF16), wo_ref[...])
    o_ref[...] = x_ref[...] + _rms(out, ln_ref[...])


def _merge(ya, yb, proj, x2, wa, wb, wo, ln, tm):
    t, d = x2.shape
    gate0 = 7
    const = lambda i: (0, 0)
    return pl.pallas_call(
        _merge_kernel,
        grid=(t // tm,),
        in_specs=[
            pl.BlockSpec((tm, d), lambda i: (i, 0)),
            pl.BlockSpec((tm, d), lambda i: (i, 0)),
            pl.BlockSpec((tm, d), lambda i: (i, gate0)),
            pl.BlockSpec((tm, d), lambda i: (i, gate0 + 1)),
            pl.BlockSpec((tm, d), lambda i: (i, 0)),
            pl.BlockSpec((d, d), const), pl.BlockSpec((d, d), const), pl.BlockSpec((d, d), const),
            pl.BlockSpec((1, d), const),
        ],
        out_specs=pl.BlockSpec((tm, d), lambda i: (i, 0)),
        out_shape=jax.ShapeDtypeStruct((t, d), F32),
        compiler_params=pltpu.CompilerParams(
            dimension_semantics=("arbitrary",), vmem_limit_bytes=VMEM_LIMIT),
        name="merge",
    )(ya, yb, proj, proj, x2, wa, wb, wo, ln)


def _ffn_ple_kernel(h_ref, p_ref, ln1_ref, wg_ref, wu_ref, wd_ref, ln2_ref,
                    wp_ref, lnp_ref, wpg_ref, o_ref):
    h1 = h_ref[...]
    f = _rms(h1, ln1_ref[...]).astype(BF16)
    act = (_silu(_dot(f, wg_ref[...])) * _dot(f, wu_ref[...])).astype(BF16)
    h2 = h1 + _rms(_dot(act, wd_ref[...]), ln2_ref[...])
    e = _rms(_dot(p_ref[...].astype(BF16), wp_ref[...]), lnp_ref[...])
    o_ref[...] = h2 + _sigmoid(_dot(h2.astype(BF16), wpg_ref[...])) * e


def _ffn_ple(h1, p2, ln1, wg, wu, wd, ln2, wp, lnp, wpg, tm):
    t, d = h1.shape
    dff = wg.shape[1]
    dp = p2.shape[1]
    const = lambda i: (0, 0)
    once = pl.Buffered(1)

    def resident(shape):
        return pl.BlockSpec(shape, const, pipeline_mode=once)

    return pl.pallas_call(
        _ffn_ple_kernel,
        grid=(t // tm,),
        in_specs=[
            pl.BlockSpec((tm, d), lambda i: (i, 0)),
            pl.BlockSpec((tm, dp), lambda i: (i, 0)),
            resident((1, d)), resident((d, dff)), resident((d, dff)), resident((dff, d)),
            resident((1, d)), resident((dp, d)), resident((1, d)), resident((d, d)),
        ],
        out_specs=pl.BlockSpec((tm, d), lambda i: (i, 0)),
        out_shape=jax.ShapeDtypeStruct((t, d), F32),
        compiler_params=pltpu.CompilerParams(
            dimension_semantics=("arbitrary",), vmem_limit_bytes=VMEM_LIMIT),
        name="ffn_ple",
    )(h1, p2, ln1, wg, wu, wd, ln2, wp, lnp, wpg)


def _rope_tables(seq):
    inv = 1.0 / (ROPE_THETA ** (jnp.arange(0, HEAD_DIM, 2, dtype=F32) / HEAD_DIM))
    ang = jnp.arange(seq, dtype=F32)[:, None] * inv[None, :]
    cos, sin = jnp.cos(ang), jnp.sin(ang)
    return jnp.concatenate([cos, cos], axis=1), jnp.concatenate([-sin, sin], axis=1)


def _layer(h, p_i, ln_pre_mix, w_in, conv_w, a_log, dt_bias, gdn_norm, w_proj_a, w_proj_b,
           w_out, ln_post_mix, ln_pre_ffn, w_ffn_gate, w_ffn_up, w_ffn_down, ln_post_ffn,
           w_ple, ln_ple, w_ple_gate):
    b, s, d = h.shape
    t = b * s
    width = N_HEADS * HEAD_DIM
    assert d == width and s % MOBA_BLOCK == 0 and s % GDN_ROWS == 0
    x2 = h.reshape(t, d)
    row = lambda v: v.reshape(1, -1).astype(F32)

    n_small = 2 * N_HEADS
    c0 = 7 * width
    w_main = jnp.concatenate([w_in[:, :c0], w_in[:, c0 + n_small:]], axis=1).astype(BF16)
    w_small = jnp.pad(w_in[:, c0:c0 + n_small], ((0, 0), (0, HEAD_DIM - n_small))).astype(BF16)
    cosf, sinf = _rope_tables(s)
    tm_in = min(1024, s)
    proj, ab, abt = _inproj(x2, row(ln_pre_mix), w_main, w_small, w_small.T, cosf, sinf, s, tm_in)

    y_a = _moba(proj, b, s)

    pad_heads = lambda v: jnp.pad(v.astype(F32), (N_HEADS, HEAD_DIM - 2 * N_HEADS))
    alog_p, dt_p = pad_heads(a_log), pad_heads(dt_bias)
    colf, rowf = _gdn_gates(ab, abt, alog_p.reshape(1, -1), dt_p.reshape(1, -1),
                            alog_p.reshape(-1, 1), dt_p.reshape(-1, 1))
    conv_w3 = conv_w.astype(F32).reshape(CONV_WIDTH, 3, width).transpose(1, 0, 2)
    y_b = _gdn(proj, colf, rowf, conv_w3, row(gdn_norm), b, s)

    tm = min(512, t)
    h1 = _merge(y_a, y_b, proj, x2, w_proj_a.astype(BF16), w_proj_b.astype(BF16),
                w_out.astype(BF16), row(ln_post_mix), tm)
    out = _ffn_ple(h1, p_i.reshape(t, -1), row(ln_pre_ffn), w_ffn_gate.astype(BF16),
                   w_ffn_up.astype(BF16), w_ffn_down.astype(BF16), row(ln_post_ffn),
                   w_ple.astype(BF16), row(ln_ple), w_ple_gate.astype(BF16), tm)
    return out.reshape(b, s, d)


def kernel(x, p, ln_pre_mix, w_in, conv_w, a_log, dt_bias, gdn_norm, w_proj_a, w_proj_b, w_out,
           ln_post_mix, ln_pre_ffn, w_ffn_gate, w_ffn_up, w_ffn_down, ln_post_ffn, w_ple, ln_ple,
           w_ple_gate):
    h = x
    for i in range(p.shape[0]):
        h = _layer(h, p[i], ln_pre_mix[i], w_in[i], conv_w[i], a_log[i], dt_bias[i], gdn_norm[i],
                   w_proj_a[i], w_proj_b[i], w_out[i], ln_post_mix[i], ln_pre_ffn[i],
                   w_ffn_gate[i], w_ffn_up[i], w_ffn_down[i], ln_post_ffn[i], w_ple[i],
                   ln_ple[i], w_ple_gate[i])
    return h
```

```python
import jax
import jax.numpy as jnp
from jax import lax
from jax.experimental import pallas as pl
from jax.experimental.pallas import tpu as pltpu

F32 = jnp.float32
BF16 = jnp.bfloat16

HEAD_DIM = 128
N_HEADS = 8
MOBA_BLOCK = 256
MOBA_TOPK = 3
MOBA_HEADS = 4
MOBA_PAST_STEP = 4
GDN_CHUNK = 64
GDN_ROWS = 256
GDN_HEADS = 4
CONV_WIDTH = 4
EPS = 1e-6
ROPE_THETA = 10000.0
NEG_BIG = -1e30
LOG2_E = 1.4426950408889634
VMEM_LIMIT = 56 * 1024 * 1024

_NT = (((1,), (1,)), ((), ()))
_TN = (((0,), (0,)), ((), ()))


def _dot(a, b):
    return jnp.dot(a, b, preferred_element_type=F32)


def _dot_nt(a, b):
    return lax.dot_general(a, b, _NT, preferred_element_type=F32)


def _dot_tn(a, b):
    return lax.dot_general(a, b, _TN, preferred_element_type=F32)


def _rms(x, w):
    return x * lax.rsqrt(jnp.mean(x * x, axis=-1, keepdims=True) + EPS) * w


def _sigmoid(x):
    return 1.0 / (1.0 + jnp.exp(-x))


def _silu(x):
    return x * _sigmoid(x)


def _softplus(x):
    return jnp.maximum(x, 0.0) + jnp.log1p(jnp.exp(-jnp.abs(x)))


def _inproj_kernel(x_ref, ln_ref, w_ref, wvt_ref, ws_ref, wst_ref, cos_ref, sin_ref,
                   o_ref, vt_ref, ab_ref, abt_ref, u_ref):
    j = pl.program_id(1)

    @pl.when(j == 0)
    def _():
        u = _rms(x_ref[...], ln_ref[...]).astype(BF16)
        u_ref[...] = u
        ab_ref[...] = _dot(u, ws_ref[...])
        abt_ref[...] = _dot_nt(wst_ref[...], u)
        vt_ref[...] = _dot_nt(wvt_ref[...], u).astype(BF16)

    acc = _dot(u_ref[...], w_ref[...])

    @pl.when(j < 2)
    def _():
        cosf = cos_ref[...]
        sinf = sin_ref[...]
        qscale = jnp.where(j == 0, LOG2_E * HEAD_DIM ** -0.5, 1.0).astype(F32)
        for h in range(N_HEADS):
            seg = acc[:, h * HEAD_DIM:(h + 1) * HEAD_DIM]
            rot = seg * cosf + pltpu.roll(seg, HEAD_DIM // 2, axis=1) * sinf
            o_ref[:, h * HEAD_DIM:(h + 1) * HEAD_DIM] = (rot * qscale).astype(BF16)

    @pl.when(j >= 2)
    def _():
        o_ref[...] = acc.astype(BF16)


def _inproj(x2, ln, w_main, w_vt, w_small, w_small_t, cosf, sinf, seq, tm):
    t, d = x2.shape
    n_main = w_main.shape[1]
    n_small = w_small.shape[1]
    tn = N_HEADS * HEAD_DIM
    n_seq_tiles = seq // tm
    return pl.pallas_call(
        _inproj_kernel,
        grid=(t // tm, n_main // tn),
        in_specs=[
            pl.BlockSpec((tm, d), lambda i, j: (i, 0)),
            pl.BlockSpec((1, d), lambda i, j: (0, 0)),
            pl.BlockSpec((d, tn), lambda i, j: (0, j)),
            pl.BlockSpec((tn, d), lambda i, j: (0, 0)),
            pl.BlockSpec((d, n_small), lambda i, j: (0, 0)),
            pl.BlockSpec((n_small, d), lambda i, j: (0, 0)),
            pl.BlockSpec((tm, HEAD_DIM), lambda i, j: (i % n_seq_tiles, 0)),
            pl.BlockSpec((tm, HEAD_DIM), lambda i, j: (i % n_seq_tiles, 0)),
        ],
        out_specs=[
            pl.BlockSpec((tm, tn), lambda i, j: (i, j)),
            pl.BlockSpec((tn, tm), lambda i, j: (0, i)),
            pl.BlockSpec((tm, n_small), lambda i, j: (i, 0)),
            pl.BlockSpec((n_small, tm), lambda i, j: (0, i)),
        ],
        out_shape=[
            jax.ShapeDtypeStruct((t, n_main), BF16),
            jax.ShapeDtypeStruct((tn, t), BF16),
            jax.ShapeDtypeStruct((t, n_small), F32),
            jax.ShapeDtypeStruct((n_small, t), F32),
        ],
        scratch_shapes=[pltpu.VMEM((tm, d), BF16)],
        compiler_params=pltpu.CompilerParams(
            dimension_semantics=("arbitrary", "arbitrary"), vmem_limit_bytes=VMEM_LIMIT),
        name="inproj",
    )(x2, ln, w_main, w_vt, w_small, w_small_t, cosf, sinf)


def _moba_kernel(q_ref, kd_ref, vtd_ref, k_ref, vt_ref, oh_ref, o_ref, kmh_ref, kml_ref, s_ref):
    i = pl.program_id(2)
    blk = MOBA_BLOCK
    n_blocks = k_ref.shape[0] // blk
    heads = q_ref.shape[1] // HEAD_DIM
    hsl = lambda g: slice(g * HEAD_DIM, (g + 1) * HEAD_DIM)

    @pl.when(i == 0)
    def _():
        for g in range(heads):
            km = jnp.mean(k_ref[:, hsl(g)].astype(F32).reshape(n_blocks, blk, HEAD_DIM), axis=1)
            hi = km.astype(BF16)
            kmh_ref[g] = hi
            kml_ref[g] = (km - hi.astype(F32)).astype(BF16)

    def attend(n_past):
        sub = lax.broadcasted_iota(jnp.int32, (n_blocks, blk), 0)
        sub_f = sub.astype(F32)
        past = sub < i
        krow = lax.broadcasted_iota(jnp.int32, (blk, blk), 0)
        qcol = lax.broadcasted_iota(jnp.int32, (blk, blk), 1)
        causal = krow <= qcol
        nk = n_past * blk
        halves = ((0, nk // 2), (nk // 2, nk)) if n_past else ()
        for g in range(heads):
            q = q_ref[:, hsl(g)]
            s_d = jnp.where(causal, _dot_nt(kd_ref[:, hsl(g)], q), -jnp.inf)
            m = jnp.max(s_d, axis=0, keepdims=True)
            if n_past:
                gate = _dot_nt(kmh_ref[g], q) + _dot_nt(kml_ref[g], q)
                gsel = jnp.where(past, gate, -jnp.inf)
                sel = jnp.zeros((n_blocks, blk), jnp.bool_)
                for _ in range(MOBA_TOPK):
                    mx = jnp.max(gsel, axis=0, keepdims=True)
                    idx = jnp.min(jnp.where(gsel == mx, sub_f, float(n_blocks)),
                                  axis=0, keepdims=True)
                    hit = sub_f == idx
                    sel = jnp.logical_or(sel, hit)
                    gsel = jnp.where(hit, -jnp.inf, gsel)
                keep = jnp.logical_and(sel, past)
                bias_t = jnp.concatenate(
                    [jnp.where(keep, 0.0, NEG_BIG),
                     jnp.zeros((HEAD_DIM - n_blocks, blk), F32)], axis=0)
                q_aug = jnp.concatenate([q, bias_t.T.astype(BF16)], axis=1)
                for lo, hi in halves:
                    k_aug = jnp.concatenate([k_ref[lo:hi, hsl(g)], oh_ref[lo:hi, :]], axis=1)
                    s_ref[g, lo:hi, :] = _dot_nt(k_aug, q_aug)
                m = jnp.maximum(m, jnp.max(s_ref[g, 0:nk, :], axis=0, keepdims=True))
            p_d = jnp.exp2(s_d - m)
            l = jnp.sum(p_d, axis=0, keepdims=True)
            acc = _dot(vtd_ref[hsl(g), :], p_d.astype(BF16))
            for lo, hi in halves:
                p = jnp.exp2(s_ref[g, lo:hi, :] - m)
                l = l + jnp.sum(p, axis=0, keepdims=True)
                acc = acc + _dot(vt_ref[hsl(g), lo:hi], p.astype(BF16))
            o_ref[:, hsl(g)] = (acc * (1.0 / l)).T.astype(BF16)

    step = MOBA_PAST_STEP
    for n_past in range(0, n_blocks + step, step):
        n_past = min(n_past, n_blocks)
        lo = n_past - step + 1 if n_past else 0
        pl.when(jnp.logical_and(i >= lo, i <= n_past))(lambda n=n_past: attend(n))
        if n_past == n_blocks:
            break


def _moba(proj, vt, onehot, batch, seq):
    t = proj.shape[0]
    nb = seq // MOBA_BLOCK
    g = MOBA_HEADS
    gw = g * HEAD_DIM
    k0 = N_HEADS // g
    return pl.pallas_call(
        _moba_kernel,
        grid=(batch, N_HEADS // g, nb),
        in_specs=[
            pl.BlockSpec((MOBA_BLOCK, gw), lambda b, h, i: (b * nb + i, h)),
            pl.BlockSpec((MOBA_BLOCK, gw), lambda b, h, i: (b * nb + i, k0 + h)),
            pl.BlockSpec((gw, MOBA_BLOCK), lambda b, h, i: (h, b * nb + i)),
            pl.BlockSpec((seq, gw), lambda b, h, i: (b, k0 + h)),
            pl.BlockSpec((gw, seq), lambda b, h, i: (h, b)),
            pl.BlockSpec((seq, HEAD_DIM), lambda b, h, i: (0, 0)),
        ],
        out_specs=pl.BlockSpec((MOBA_BLOCK, gw), lambda b, h, i: (b * nb + i, h)),
        out_shape=jax.ShapeDtypeStruct((t, N_HEADS * HEAD_DIM), BF16),
        scratch_shapes=[pltpu.VMEM((g, nb, HEAD_DIM), BF16),
                        pltpu.VMEM((g, nb, HEAD_DIM), BF16),
                        pltpu.VMEM((g, seq, MOBA_BLOCK), F32)],
        compiler_params=pltpu.CompilerParams(
            dimension_semantics=("arbitrary", "arbitrary", "arbitrary"),
            vmem_limit_bytes=VMEM_LIMIT),
        name="moba",
    )(proj, proj, vt, proj, vt, onehot)


def _chunk_masks(n):
    row = lax.broadcasted_iota(jnp.int32, (n, n), 0)
    col = lax.broadcasted_iota(jnp.int32, (n, n), 1)
    shift = GDN_CHUNK.bit_length() - 1
    same = (row >> shift) == (col >> shift)
    return row, col, same


def _gdn_gates_kernel(ab_ref, abt_ref, alog_row_ref, dt_row_ref, alog_col_ref, dt_col_ref,
                      col_ref, rowf_ref):
    r = ab_ref.shape[0]
    g_heads = GDN_HEADS
    hi = lax.Precision.HIGHEST
    row, col, same = _chunk_masks(r)
    tril = jnp.where(jnp.logical_and(same, col <= row), 1.0, 0.0).astype(F32)
    ones = jnp.where(same, 1.0, 0.0).astype(F32)

    ab = ab_ref[...]
    lane = lax.broadcasted_iota(jnp.int32, ab.shape, 1)
    is_b = lane < g_heads
    is_a = jnp.logical_and(lane >= g_heads, lane < 2 * g_heads)
    g = -jnp.exp(alog_row_ref[...]) * _softplus(ab + dt_row_ref[...])
    g = jnp.where(is_a, g, 0.0)
    gc = jnp.dot(tril, g, precision=hi, preferred_element_type=F32)
    glast = jnp.dot(ones, g, precision=hi, preferred_element_type=F32)
    out = jnp.where(is_b, _sigmoid(ab), gc) + pltpu.roll(glast, g_heads, axis=1)
    col_ref[...] = out

    abt = abt_ref[...]
    srow = lax.broadcasted_iota(jnp.int32, abt.shape, 0)
    gt = -jnp.exp(alog_col_ref[...]) * _softplus(abt + dt_col_ref[...])
    gt = jnp.where(jnp.logical_and(srow >= g_heads, srow < 2 * g_heads), gt, 0.0)
    triu = jnp.where(jnp.logical_and(same, row <= col), 1.0, 0.0).astype(F32)
    rowf_ref[...] = jnp.dot(gt, triu, precision=hi, preferred_element_type=F32)


def _gdn_gates(ab, abt, alog_row, dt_row, alog_col, dt_col):
    t = ab.shape[0]
    r = GDN_ROWS
    n_groups = ab.shape[1] // HEAD_DIM
    return pl.pallas_call(
        _gdn_gates_kernel,
        grid=(t // r, n_groups),
        in_specs=[
            pl.BlockSpec((r, HEAD_DIM), lambda i, hg: (i, hg)),
            pl.BlockSpec((HEAD_DIM, r), lambda i, hg: (hg, i)),
            pl.BlockSpec((1, HEAD_DIM), lambda i, hg: (0, hg)),
            pl.BlockSpec((1, HEAD_DIM), lambda i, hg: (0, hg)),
            pl.BlockSpec((HEAD_DIM, 1), lambda i, hg: (hg, 0)),
            pl.BlockSpec((HEAD_DIM, 1), lambda i, hg: (hg, 0)),
        ],
        out_specs=[
            pl.BlockSpec((r, HEAD_DIM), lambda i, hg: (i, hg)),
            pl.BlockSpec((HEAD_DIM, r), lambda i, hg: (hg, i)),
        ],
        out_shape=[
            jax.ShapeDtypeStruct(ab.shape, F32),
            jax.ShapeDtypeStruct(abt.shape, F32),
        ],
        compiler_params=pltpu.CompilerParams(dimension_semantics=("arbitrary", "arbitrary")),
        name="gdn_gates",
    )(ab, abt, alog_row, dt_row, alog_col, dt_col)


def _gdn_kernel(q_ref, k_ref, v_ref, z_ref, colf_ref, rowf_ref, cw_ref, gn_ref,
                o_ref, xe_ref, state_ref):
    sblk = pl.program_id(2)
    r = q_ref.shape[0]
    heads = q_ref.shape[1] // HEAD_DIM
    n_chunks = r // GDN_CHUNK
    pad = 8
    hsl = lambda g: slice(g * HEAD_DIM, (g + 1) * HEAD_DIM)

    @pl.when(sblk == 0)
    def _():
        state_ref[...] = jnp.zeros_like(state_ref)
        xe_ref[:, 0:pad, :] = jnp.zeros((3, pad, heads * HEAD_DIM), F32)

    def conv_silu(c, raw_ref):
        x = raw_ref[...].astype(F32)
        xe_ref[c, pad:pad + r, :] = x
        w = cw_ref[c]
        y = x * w[CONV_WIDTH - 1:CONV_WIDTH, :]
        for d in range(1, CONV_WIDTH):
            y = y + xe_ref[c, pad - d:pad - d + r, :] * w[CONV_WIDTH - 1 - d:CONV_WIDTH - d, :]
        xe_ref[c, 0:pad, :] = x[r - pad:r, :]
        return _silu(y)

    q_all = conv_silu(0, q_ref)
    k_all = conv_silu(1, k_ref)
    v_all = conv_silu(2, v_ref)

    row, col, same = _chunk_masks(r)
    causal = jnp.logical_and(same, col <= row)
    strict = jnp.logical_and(same, col < row)
    eye = jnp.where(row == col, 1.0, 0.0).astype(F32)
    colf = colf_ref[...]
    gnorm = gn_ref[...]
    n_sq = (GDN_CHUNK - 1).bit_length() - 1

    for g in range(heads):
        q = q_all[:, hsl(g)]
        k = k_all[:, hsl(g)]
        v = v_all[:, hsl(g)]
        q = q * lax.rsqrt(jnp.sum(q * q, axis=-1, keepdims=True) + EPS) * (HEAD_DIM ** -0.5)
        k = k * lax.rsqrt(jnp.sum(k * k, axis=-1, keepdims=True) + EPS)

        beta = colf[:, g:g + 1]
        gc = colf[:, heads + g:heads + g + 1]
        glast = colf[:, 2 * heads + g:2 * heads + g + 1]
        gc_row = rowf_ref[heads + g:heads + g + 1, :]
        decay = jnp.exp(jnp.where(causal, gc - gc_row, -jnp.inf))

        kb = k * beta
        k16 = k.astype(BF16)
        a_low = jnp.where(strict, _dot_nt(kb.astype(BF16), k16) * decay, 0.0)
        attn = _dot_nt(q.astype(BF16), k16) * decay

        t_inv = eye - a_low
        a_pow = a_low
        for _ in range(n_sq):
            a16 = a_pow.astype(BF16)
            a_pow = _dot(a16, a16)
            t_inv = t_inv + _dot(t_inv.astype(BF16), a_pow.astype(BF16))

        egc = jnp.exp(gc)
        rhs = jnp.concatenate([v * beta, kb * egc], axis=1)
        sol16 = _dot(t_inv.astype(BF16), rhs.astype(BF16)).astype(BF16)
        aw_au = _dot(attn.astype(BF16), sol16)
        p_mat = (q * egc - aw_au[:, HEAD_DIM:]).astype(BF16)
        r_mat = aw_au[:, :HEAD_DIM]
        kd = (k * jnp.exp(glast - gc)).astype(BF16)
        e_chunk = jnp.exp(glast)

        state = state_ref[g]
        outs = []
        for c in range(n_chunks):
            lo, hi = c * GDN_CHUNK, (c + 1) * GDN_CHUNK
            nu_g = _dot_tn(kd[lo:hi, :], sol16[lo:hi, :])
            s16 = state.astype(BF16)
            outs.append(_dot(p_mat[lo:hi, :], s16) + r_mat[lo:hi, :])
            state = (e_chunk[lo:lo + 1, :] * state
                     - _dot(nu_g[:, HEAD_DIM:].astype(BF16), s16) + nu_g[:, :HEAD_DIM])
        state_ref[g] = state
        o = jnp.concatenate(outs, axis=0)

        z = z_ref[:, hsl(g)].astype(F32)
        o_ref[:, hsl(g)] = (_rms(o, gnorm) * _silu(z)).astype(BF16)


def _gdn(proj, colf, rowf, conv_w3, gdn_norm, batch, seq):
    t = proj.shape[0]
    r = GDN_ROWS
    ns = seq // r
    g = GDN_HEADS
    gw = g * HEAD_DIM
    n_groups = N_HEADS // g
    qb0, kb0, vb0, z0 = (n * n_groups for n in (2, 3, 4, 5))

    def tok(off):
        return pl.BlockSpec((r, gw), lambda b, h, s: (b * ns + s, off + h))

    return pl.pallas_call(
        _gdn_kernel,
        grid=(batch, n_groups, ns),
        in_specs=[
            tok(qb0), tok(kb0), tok(vb0), tok(z0),
            pl.BlockSpec((r, HEAD_DIM), lambda b, h, s: (b * ns + s, h)),
            pl.BlockSpec((HEAD_DIM, r), lambda b, h, s: (h, b * ns + s)),
            pl.BlockSpec((3, CONV_WIDTH, gw), lambda b, h, s: (0, 0, h)),
            pl.BlockSpec((1, HEAD_DIM), lambda b, h, s: (0, 0)),
        ],
        out_specs=pl.BlockSpec((r, gw), lambda b, h, s: (b * ns + s, h)),
        out_shape=jax.ShapeDtypeStruct((t, N_HEADS * HEAD_DIM), BF16),
        scratch_shapes=[pltpu.VMEM((3, r + 8, gw), F32),
                        pltpu.VMEM((g, HEAD_DIM, HEAD_DIM), F32)],
        compiler_params=pltpu.CompilerParams(
            dimension_semantics=("arbitrary", "arbitrary", "arbitrary"),
            vmem_limit_bytes=VMEM_LIMIT),
        name="gdn",
    )(proj, proj, proj, proj, colf, rowf, conv_w3, gdn_norm)


def _merge_kernel(ya_ref, yb_ref, ga_ref, gb_ref, x_ref, wa_ref, wb_ref, wo_ref, ln_ref, o_ref):
    merged = (_sigmoid(ga_ref[...].astype(F32)) * _dot(ya_ref[...], wa_ref[...])
              + _sigmoid(gb_ref[...].astype(F32)) * _dot(yb_ref[...], wb_ref[...]))
    out = _dot(merged.astype(BF16), wo_ref[...])
    o_ref[...] = x_ref[...] + _rms(out, ln_ref[...])


def _merge(ya, yb, proj, x2, wa, wb, wo, ln, tm):
    t, d = x2.shape
    gate0 = 6
    const = lambda i: (0, 0)
    return pl.pallas_call(
        _merge_kernel,
        grid=(t // tm,),
        in_specs=[
            pl.BlockSpec((tm, d), lambda i: (i, 0)),
            pl.BlockSpec((tm, d), lambda i: (i, 0)),
            pl.BlockSpec((tm, d), lambda i: (i, gate0)),
            pl.BlockSpec((tm, d), lambda i: (i, gate0 + 1)),
            pl.BlockSpec((tm, d), lambda i: (i, 0)),
            pl.BlockSpec((d, d), const), pl.BlockSpec((d, d), const), pl.BlockSpec((d, d), const),
            pl.BlockSpec((1, d), const),
        ],
        out_specs=pl.BlockSpec((tm, d), lambda i: (i, 0)),
        out_shape=jax.ShapeDtypeStruct((t, d), F32),
        compiler_params=pltpu.CompilerParams(
            dimension_semantics=("arbitrary",), vmem_limit_bytes=VMEM_LIMIT),
        name="merge",
    )(ya, yb, proj, proj, x2, wa, wb, wo, ln)


def _ffn_ple_kernel(h_ref, p_ref, ln1_ref, wg_ref, wu_ref, wd_ref, ln2_ref,
                    wp_ref, lnp_ref, wpg_ref, o_ref):
    h1 = h_ref[...]
    f = _rms(h1, ln1_ref[...]).astype(BF16)
    act = (_silu(_dot(f, wg_ref[...])) * _dot(f, wu_ref[...])).astype(BF16)
    h2 = h1 + _rms(_dot(act, wd_ref[...]), ln2_ref[...])
    e = _rms(_dot(p_ref[...].astype(BF16), wp_ref[...]), lnp_ref[...])
    o_ref[...] = h2 + _sigmoid(_dot(h2.astype(BF16), wpg_ref[...])) * e


def _ffn_ple(h1, p2, ln1, wg, wu, wd, ln2, wp, lnp, wpg, tm):
    t, d = h1.shape
    dff = wg.shape[1]
    dp = p2.shape[1]
    const = lambda i: (0, 0)
    once = pl.Buffered(1)

    def resident(shape):
        return pl.BlockSpec(shape, const, pipeline_mode=once)

    return pl.pallas_call(
        _ffn_ple_kernel,
        grid=(t // tm,),
        in_specs=[
            pl.BlockSpec((tm, d), lambda i: (i, 0)),
            pl.BlockSpec((tm, dp), lambda i: (i, 0)),
            resident((1, d)), resident((d, dff)), resident((d, dff)), resident((dff, d)),
            resident((1, d)), resident((dp, d)), resident((1, d)), resident((d, d)),
        ],
        out_specs=pl.BlockSpec((tm, d), lambda i: (i, 0)),
        out_shape=jax.ShapeDtypeStruct((t, d), F32),
        compiler_params=pltpu.CompilerParams(
            dimension_semantics=("arbitrary",), vmem_limit_bytes=VMEM_LIMIT),
        name="ffn_ple",
    )(h1, p2, ln1, wg, wu, wd, ln2, wp, lnp, wpg)


def _rope_tables(seq):
    inv = 1.0 / (ROPE_THETA ** (jnp.arange(0, HEAD_DIM, 2, dtype=F32) / HEAD_DIM))
    ang = jnp.arange(seq, dtype=F32)[:, None] * inv[None, :]
    cos, sin = jnp.cos(ang), jnp.sin(ang)
    return jnp.concatenate([cos, cos], axis=1), jnp.concatenate([-sin, sin], axis=1)


def _group_lanes(offset):
    h = jnp.arange(N_HEADS)
    return (h // GDN_HEADS) * HEAD_DIM + h % GDN_HEADS + offset


def _layer(h, p_i, ln_pre_mix, w_in, conv_w, a_log, dt_bias, gdn_norm, w_proj_a, w_proj_b,
           w_out, ln_post_mix, ln_pre_ffn, w_ffn_gate, w_ffn_up, w_ffn_down, ln_post_ffn,
           w_ple, ln_ple, w_ple_gate):
    b, s, d = h.shape
    t = b * s
    width = N_HEADS * HEAD_DIM
    assert d == width and s % MOBA_BLOCK == 0 and s % GDN_ROWS == 0
    x2 = h.reshape(t, d)
    row = lambda v: v.reshape(1, -1).astype(F32)

    c_small = 7 * width
    c_gate = c_small + 2 * N_HEADS
    w_main = jnp.concatenate(
        [w_in[:, :2 * width], w_in[:, 3 * width:c_small], w_in[:, c_gate:]], axis=1).astype(BF16)
    w_vt = w_in[:, 2 * width:3 * width].T.astype(BF16)
    n_small = (N_HEADS // GDN_HEADS) * HEAD_DIM
    lanes_b, lanes_a = _group_lanes(0), _group_lanes(GDN_HEADS)
    w_small = (jnp.zeros((d, n_small), F32)
               .at[:, lanes_b].set(w_in[:, c_small:c_small + N_HEADS])
               .at[:, lanes_a].set(w_in[:, c_small + N_HEADS:c_gate])).astype(BF16)
    cosf, sinf = _rope_tables(s)
    tm_in = min(1024, s)
    proj, vt, ab, abt = _inproj(x2, row(ln_pre_mix), w_main, w_vt, w_small, w_small.T,
                                 cosf, sinf, s, tm_in)

    key_block = jnp.arange(s, dtype=jnp.int32)[:, None] // MOBA_BLOCK
    onehot = (key_block == jnp.arange(HEAD_DIM, dtype=jnp.int32)[None, :]).astype(BF16)
    y_a = _moba(proj, vt, onehot, b, s)

    alog_p = jnp.zeros((n_small,), F32).at[lanes_a].set(a_log.astype(F32))
    dt_p = jnp.zeros((n_small,), F32).at[lanes_a].set(dt_bias.astype(F32))
    colf, rowf = _gdn_gates(ab, abt, alog_p.reshape(1, -1), dt_p.reshape(1, -1),
                            alog_p.reshape(-1, 1), dt_p.reshape(-1, 1))
    conv_w3 = conv_w.astype(F32).reshape(CONV_WIDTH, 3, width).transpose(1, 0, 2)
    y_b = _gdn(proj, colf, rowf, conv_w3, row(gdn_norm), b, s)

    tm = min(512, t)
    h1 = _merge(y_a, y_b, proj, x2, w_proj_a.astype(BF16), w_proj_b.astype(BF16),
                w_out.astype(BF16), row(ln_post_mix), tm)
    out = _ffn_ple(h1, p_i.reshape(t, -1), row(ln_pre_ffn), w_ffn_gate.astype(BF16),
                   w_ffn_up.astype(BF16), w_ffn_down.astype(BF16), row(ln_post_ffn),
                   w_ple.astype(BF16), row(ln_ple), w_ple_gate.astype(BF16), tm)
    return out.reshape(b, s, d)


def kernel(x, p, ln_pre_mix, w_in, conv_w, a_log, dt_bias, gdn_norm, w_proj_a, w_proj_b, w_out,
           ln_post_mix, ln_pre_ffn, w_ffn_gate, w_ffn_up, w_ffn_down, ln_post_ffn, w_ple, ln_ple,
           w_ple_gate):
    h = x
    for i in range(p.shape[0]):
        h = _layer(h, p[i], ln_pre_mix[i], w_in[i], conv_w[i], a_log[i], dt_bias[i], gdn_norm[i],
                   w_proj_a[i], w_proj_b[i], w_out[i], ln_post_mix[i], ln_pre_ffn[i],
                   w_ffn_gate[i], w_ffn_up[i], w_ffn_down[i], ln_post_ffn[i], w_ple[i],
                   ln_ple[i], w_ple_gate[i])
    return h
```

```python
import jax
import jax.numpy as jnp
from jax import lax
from jax.experimental import pallas as pl
from jax.experimental.pallas import tpu as pltpu

F32 = jnp.float32
BF16 = jnp.bfloat16

HEAD_DIM = 128
N_HEADS = 8
MOBA_BLOCK = 256
MOBA_TOPK = 3
MOBA_HEADS = 4
MOBA_PAST_STEP = 4
GDN_CHUNK = 64
GDN_ROWS = 256
CONV_WIDTH = 4
EPS = 1e-6
ROPE_THETA = 10000.0
NEG_BIG = -1e30
LOG2_E = 1.4426950408889634
VMEM_LIMIT = 56 * 1024 * 1024

_NT = (((1,), (1,)), ((), ()))
_TN = (((0,), (0,)), ((), ()))


def _dot(a, b):
    return jnp.dot(a, b, preferred_element_type=F32)


def _dot_nt(a, b):
    return lax.dot_general(a, b, _NT, preferred_element_type=F32)


def _dot_tn(a, b):
    return lax.dot_general(a, b, _TN, preferred_element_type=F32)


def _rms(x, w):
    return x * lax.rsqrt(jnp.mean(x * x, axis=-1, keepdims=True) + EPS) * w


def _sigmoid(x):
    return 1.0 / (1.0 + jnp.exp(-x))


def _silu(x):
    return x * _sigmoid(x)


def _softplus(x):
    return jnp.maximum(x, 0.0) + jnp.log1p(jnp.exp(-jnp.abs(x)))


def _inproj_kernel(x_ref, ln_ref, w_ref, wvt_ref, ws_ref, wst_ref, cos_ref, sin_ref,
                   o_ref, vt_ref, ab_ref, abt_ref, u_ref):
    j = pl.program_id(1)

    @pl.when(j == 0)
    def _():
        u = _rms(x_ref[...], ln_ref[...]).astype(BF16)
        u_ref[...] = u
        ab_ref[...] = _dot(u, ws_ref[...])
        abt_ref[...] = _dot_nt(wst_ref[...], u)
        vt_ref[...] = _dot_nt(wvt_ref[...], u).astype(BF16)

    acc = _dot(u_ref[...], w_ref[...])

    @pl.when(j < 2)
    def _():
        cosf = cos_ref[...]
        sinf = sin_ref[...]
        qscale = jnp.where(j == 0, LOG2_E * HEAD_DIM ** -0.5, 1.0).astype(F32)
        for h in range(N_HEADS):
            seg = acc[:, h * HEAD_DIM:(h + 1) * HEAD_DIM]
            rot = seg * cosf + pltpu.roll(seg, HEAD_DIM // 2, axis=1) * sinf
            o_ref[:, h * HEAD_DIM:(h + 1) * HEAD_DIM] = (rot * qscale).astype(BF16)

    @pl.when(j >= 2)
    def _():
        o_ref[...] = acc.astype(BF16)


def _inproj(x2, ln, w_main, w_vt, w_small, w_small_t, cosf, sinf, seq, tm):
    t, d = x2.shape
    n_main = w_main.shape[1]
    n_small = w_small.shape[1]
    tn = N_HEADS * HEAD_DIM
    n_seq_tiles = seq // tm
    return pl.pallas_call(
        _inproj_kernel,
        grid=(t // tm, n_main // tn),
        in_specs=[
            pl.BlockSpec((tm, d), lambda i, j: (i, 0)),
            pl.BlockSpec((1, d), lambda i, j: (0, 0)),
            pl.BlockSpec((d, tn), lambda i, j: (0, j)),
            pl.BlockSpec((tn, d), lambda i, j: (0, 0)),
            pl.BlockSpec((d, n_small), lambda i, j: (0, 0)),
            pl.BlockSpec((n_small, d), lambda i, j: (0, 0)),
            pl.BlockSpec((tm, HEAD_DIM), lambda i, j: (i % n_seq_tiles, 0)),
            pl.BlockSpec((tm, HEAD_DIM), lambda i, j: (i % n_seq_tiles, 0)),
        ],
        out_specs=[
            pl.BlockSpec((tm, tn), lambda i, j: (i, j)),
            pl.BlockSpec((tn, tm), lambda i, j: (0, i)),
            pl.BlockSpec((tm, n_small), lambda i, j: (i, 0)),
            pl.BlockSpec((n_small, tm), lambda i, j: (0, i)),
        ],
        out_shape=[
            jax.ShapeDtypeStruct((t, n_main), BF16),
            jax.ShapeDtypeStruct((tn, t), BF16),
            jax.ShapeDtypeStruct((t, n_small), F32),
            jax.ShapeDtypeStruct((n_small, t), F32),
        ],
        scratch_shapes=[pltpu.VMEM((tm, d), BF16)],
        compiler_params=pltpu.CompilerParams(
            dimension_semantics=("arbitrary", "arbitrary"), vmem_limit_bytes=VMEM_LIMIT),
        name="inproj",
    )(x2, ln, w_main, w_vt, w_small, w_small_t, cosf, sinf)


def _moba_kernel(q_ref, kd_ref, vtd_ref, k_ref, vt_ref, oh_ref, o_ref, kmh_ref, kml_ref, s_ref):
    i = pl.program_id(2)
    blk = MOBA_BLOCK
    n_blocks = k_ref.shape[0] // blk
    heads = q_ref.shape[1] // HEAD_DIM
    hsl = lambda g: slice(g * HEAD_DIM, (g + 1) * HEAD_DIM)

    @pl.when(i == 0)
    def _():
        for g in range(heads):
            km = jnp.mean(k_ref[:, hsl(g)].astype(F32).reshape(n_blocks, blk, HEAD_DIM), axis=1)
            hi = km.astype(BF16)
            kmh_ref[g] = hi
            kml_ref[g] = (km - hi.astype(F32)).astype(BF16)

    def attend(n_past):
        sub = lax.broadcasted_iota(jnp.int32, (n_blocks, blk), 0)
        sub_f = sub.astype(F32)
        past = sub < i
        krow = lax.broadcasted_iota(jnp.int32, (blk, blk), 0)
        qcol = lax.broadcasted_iota(jnp.int32, (blk, blk), 1)
        causal = krow <= qcol
        nk = n_past * blk
        halves = ((0, nk // 2), (nk // 2, nk)) if n_past else ()
        for g in range(heads):
            q = q_ref[:, hsl(g)]
            s_d = jnp.where(causal, _dot_nt(kd_ref[:, hsl(g)], q), -jnp.inf)
            m = jnp.max(s_d, axis=0, keepdims=True)
            if n_past:
                gate = _dot_nt(kmh_ref[g], q) + _dot_nt(kml_ref[g], q)
                gsel = jnp.where(past, gate, -jnp.inf)
                sel = jnp.zeros((n_blocks, blk), jnp.bool_)
                for _ in range(MOBA_TOPK):
                    mx = jnp.max(gsel, axis=0, keepdims=True)
                    idx = jnp.min(jnp.where(gsel == mx, sub_f, float(n_blocks)),
                                  axis=0, keepdims=True)
                    hit = sub_f == idx
                    sel = jnp.logical_or(sel, hit)
                    gsel = jnp.where(hit, -jnp.inf, gsel)
                keep = jnp.logical_and(sel, past)
                bias_t = jnp.concatenate(
                    [jnp.where(keep, 0.0, NEG_BIG),
                     jnp.zeros((HEAD_DIM - n_blocks, blk), F32)], axis=0)
                q_aug = jnp.concatenate([q, bias_t.T.astype(BF16)], axis=1)
                for lo, hi in halves:
                    k_aug = jnp.concatenate([k_ref[lo:hi, hsl(g)], oh_ref[lo:hi, :]], axis=1)
                    s_ref[g, lo:hi, :] = _dot_nt(k_aug, q_aug)
                m = jnp.maximum(m, jnp.max(s_ref[g, 0:nk, :], axis=0, keepdims=True))
            p_d = jnp.exp2(s_d - m)
            l = jnp.sum(p_d, axis=0, keepdims=True)
            acc = _dot(vtd_ref[hsl(g), :], p_d.astype(BF16))
            for lo, hi in halves:
                p = jnp.exp2(s_ref[g, lo:hi, :] - m)
                l = l + jnp.sum(p, axis=0, keepdims=True)
                acc = acc + _dot(vt_ref[hsl(g), lo:hi], p.astype(BF16))
            o_ref[:, hsl(g)] = (acc * (1.0 / l)).T.astype(BF16)

    step = MOBA_PAST_STEP
    for n_past in range(0, n_blocks + step, step):
        n_past = min(n_past, n_blocks)
        lo = n_past - step + 1 if n_past else 0
        pl.when(jnp.logical_and(i >= lo, i <= n_past))(lambda n=n_past: attend(n))
        if n_past == n_blocks:
            break


def _moba(proj, vt, onehot, batch, seq):
    t = proj.shape[0]
    nb = seq // MOBA_BLOCK
    g = MOBA_HEADS
    gw = g * HEAD_DIM
    k0 = N_HEADS // g
    return pl.pallas_call(
        _moba_kernel,
        grid=(batch, N_HEADS // g, nb),
        in_specs=[
            pl.BlockSpec((MOBA_BLOCK, gw), lambda b, h, i: (b * nb + i, h)),
            pl.BlockSpec((MOBA_BLOCK, gw), lambda b, h, i: (b * nb + i, k0 + h)),
            pl.BlockSpec((gw, MOBA_BLOCK), lambda b, h, i: (h, b * nb + i)),
            pl.BlockSpec((seq, gw), lambda b, h, i: (b, k0 + h)),
            pl.BlockSpec((gw, seq), lambda b, h, i: (h, b)),
            pl.BlockSpec((seq, HEAD_DIM), lambda b, h, i: (0, 0)),
        ],
        out_specs=pl.BlockSpec((MOBA_BLOCK, gw), lambda b, h, i: (b * nb + i, h)),
        out_shape=jax.ShapeDtypeStruct((t, N_HEADS * HEAD_DIM), BF16),
        scratch_shapes=[pltpu.VMEM((g, nb, HEAD_DIM), BF16),
                        pltpu.VMEM((g, nb, HEAD_DIM), BF16),
                        pltpu.VMEM((g, seq, MOBA_BLOCK), F32)],
        compiler_params=pltpu.CompilerParams(
            dimension_semantics=("arbitrary", "arbitrary", "arbitrary"),
            vmem_limit_bytes=VMEM_LIMIT),
        name="moba",
    )(proj, proj, vt, proj, vt, onehot)


def _chunk_masks(n):
    row = lax.broadcasted_iota(jnp.int32, (n, n), 0)
    col = lax.broadcasted_iota(jnp.int32, (n, n), 1)
    shift = GDN_CHUNK.bit_length() - 1
    same = (row >> shift) == (col >> shift)
    return row, col, same


def _split3(x):
    hi = x.astype(BF16)
    rest = x - hi.astype(F32)
    mid = rest.astype(BF16)
    lo = (rest - mid.astype(F32)).astype(BF16)
    return hi, mid, lo


def _gdn_gates_kernel(ab_ref, abt_ref, alog_col_ref, dt_col_ref, col_ref, rowf_ref):
    r = ab_ref.shape[0]
    nh = N_HEADS
    row, col, same = _chunk_masks(r)
    triu = jnp.where(jnp.logical_and(same, row <= col), 1.0, 0.0).astype(BF16)
    ones = jnp.where(same, 1.0, 0.0).astype(BF16)

    abt = abt_ref[...]
    srow = lax.broadcasted_iota(jnp.int32, abt.shape, 0)
    gt = -jnp.exp(alog_col_ref[...]) * _softplus(abt + dt_col_ref[...])
    gt = jnp.where(srow >= nh, gt, 0.0)
    pieces = _split3(gt)
    gc_t = sum(_dot(p, triu) for p in pieces)
    gl_t = sum(_dot(p, ones) for p in pieces)
    rowf_ref[...] = gc_t

    stack = jnp.concatenate(
        [gc_t[nh:2 * nh, :], gl_t[nh:2 * nh, :], jnp.zeros((HEAD_DIM - 2 * nh, r), F32)], axis=0)
    shifted = pltpu.roll(stack.T, nh, axis=1)
    ab = ab_ref[...]
    lane = lax.broadcasted_iota(jnp.int32, ab.shape, 1)
    col_ref[...] = jnp.where(lane < nh, _sigmoid(ab), shifted)


def _gdn_gates(ab, abt, alog_col, dt_col):
    t = ab.shape[0]
    r = GDN_ROWS
    rows = 2 * N_HEADS
    return pl.pallas_call(
        _gdn_gates_kernel,
        grid=(t // r,),
        in_specs=[
            pl.BlockSpec((r, HEAD_DIM), lambda i: (i, 0)),
            pl.BlockSpec((rows, r), lambda i: (0, i)),
            pl.BlockSpec((rows, 1), lambda i: (0, 0)),
            pl.BlockSpec((rows, 1), lambda i: (0, 0)),
        ],
        out_specs=[
            pl.BlockSpec((r, HEAD_DIM), lambda i: (i, 0)),
            pl.BlockSpec((rows, r), lambda i: (0, i)),
        ],
        out_shape=[
            jax.ShapeDtypeStruct((t, HEAD_DIM), F32),
            jax.ShapeDtypeStruct((rows, t), F32),
        ],
        compiler_params=pltpu.CompilerParams(dimension_semantics=("arbitrary",)),
        name="gdn_gates",
    )(ab, abt, alog_col, dt_col)


def _gdn_kernel(q_ref, k_ref, v_ref, z_ref, colf_ref, rowf_ref, cw_ref, gn_ref,
                o_ref, xe_ref, state_ref, cp_ref, at_ref, x_ref, qe_ref, kd_ref, colp_ref):
    sblk = pl.program_id(1)
    r = q_ref.shape[0]
    heads = q_ref.shape[1] // HEAD_DIM
    n_chunks = r // GDN_CHUNK
    n_levels = (GDN_CHUNK - 1).bit_length()
    pad = 8
    hsl = lambda g: slice(g * HEAD_DIM, (g + 1) * HEAD_DIM)
    handover = (cp_ref, at_ref, x_ref, qe_ref, kd_ref, colp_ref)

    @pl.when(jnp.logical_and(pl.program_id(0) == 0, sblk == 0))
    def _():
        for ref in handover:
            ref[...] = jnp.zeros_like(ref)
        state_ref[...] = jnp.zeros_like(state_ref)

    @pl.when(sblk == 0)
    def _():
        xe_ref[...] = jnp.zeros_like(xe_ref)

    def prepare(slot):
        def conv_silu(c, raw_ref):
            x = raw_ref[...].astype(F32)
            xe = jnp.concatenate([xe_ref[c], x], axis=0)
            w = cw_ref[c]
            y = x * w[CONV_WIDTH - 1:CONV_WIDTH, :]
            for d in range(1, CONV_WIDTH):
                y = y + pltpu.roll(xe, d, axis=0)[pad:, :] * w[CONV_WIDTH - 1 - d:CONV_WIDTH - d, :]
            xe_ref[c] = x[r - pad:r, :]
            return _silu(y)

        q_all = conv_silu(0, q_ref)
        k_all = conv_silu(1, k_ref)
        v_all = conv_silu(2, v_ref)

        row, col, same = _chunk_masks(r)
        causal = jnp.logical_and(same, col <= row)
        strict = jnp.logical_and(same, col < row)
        colf = colf_ref[...]
        colp_ref[slot] = colf
        for g in range(heads):
            q = q_all[:, hsl(g)]
            k = k_all[:, hsl(g)]
            v = v_all[:, hsl(g)]
            q = q * lax.rsqrt(jnp.sum(q * q, axis=-1, keepdims=True) + EPS) * (HEAD_DIM ** -0.5)
            k = k * lax.rsqrt(jnp.sum(k * k, axis=-1, keepdims=True) + EPS)

            beta = colf[:, g:g + 1]
            gc = colf[:, heads + g:heads + g + 1]
            glast = colf[:, 2 * heads + g:2 * heads + g + 1]
            gc_row = rowf_ref[heads + g:heads + g + 1, :]
            decay = jnp.exp(jnp.where(causal, gc - gc_row, -jnp.inf))

            kb = k * beta
            k16 = k.astype(BF16)
            cp_ref[slot, g] = jnp.where(
                strict, -_dot_nt(kb.astype(BF16), k16) * decay, 0.0).astype(BF16)
            at_ref[slot, g] = (_dot_nt(q.astype(BF16), k16) * decay).astype(BF16)
            egc = jnp.exp(gc)
            x_ref[slot, g] = jnp.concatenate([v * beta, kb * egc], axis=1)
            qe_ref[slot, g] = q * egc
            kd_ref[slot, g] = (k * jnp.exp(glast - gc)).astype(BF16)

    def finish(slot):
        hd_range = range(heads)
        c_pow = [cp_ref[slot, g] for g in hd_range]
        x = [x_ref[slot, g] for g in hd_range]
        for level in range(n_levels):
            for g in hd_range:
                x[g] = x[g] + _dot(c_pow[g], x[g].astype(BF16))
                if level + 1 < n_levels:
                    c_pow[g] = _dot(c_pow[g], c_pow[g]).astype(BF16)

        p_mat, r_mat, nu_g = [], [], []
        for g in hd_range:
            sol16 = x[g].astype(BF16)
            aw_au = _dot(at_ref[slot, g], sol16)
            p_mat.append((qe_ref[slot, g] - aw_au[:, HEAD_DIM:]).astype(BF16))
            r_mat.append(aw_au[:, :HEAD_DIM])
            kd = kd_ref[slot, g]
            nu_g.append([_dot_tn(kd[c * GDN_CHUNK:(c + 1) * GDN_CHUNK, :],
                                 sol16[c * GDN_CHUNK:(c + 1) * GDN_CHUNK, :])
                         for c in range(n_chunks)])

        colp = colp_ref[slot]
        states = [state_ref[g] for g in hd_range]
        outs = [[] for _ in hd_range]
        for c in range(n_chunks):
            lo, hi = c * GDN_CHUNK, (c + 1) * GDN_CHUNK
            for g in hd_range:
                s16 = states[g].astype(BF16)
                e_c = jnp.exp(colp[lo:lo + 1, 2 * heads + g:2 * heads + g + 1])
                outs[g].append(_dot(p_mat[g][lo:hi, :], s16) + r_mat[g][lo:hi, :])
                states[g] = (e_c * states[g] - _dot(nu_g[g][c][:, HEAD_DIM:].astype(BF16), s16)
                             + nu_g[g][c][:, :HEAD_DIM])

        gnorm = gn_ref[...]
        for g in hd_range:
            state_ref[g] = jnp.where(sblk > 0, states[g], 0.0)
            o = jnp.concatenate(outs[g], axis=0)
            z = z_ref[:, hsl(g)].astype(F32)
            o_ref[:, hsl(g)] = (_rms(o, gnorm) * _silu(z)).astype(BF16)

    for parity in range(2):
        @pl.when(sblk % 2 == parity)
        def _():
            finish(1 - parity)
            prepare(parity)


def _gdn(proj, colf, rowf, conv_w3, gdn_norm, batch, seq):
    t = proj.shape[0]
    r = GDN_ROWS
    ns = seq // r
    assert ns % 2 == 0
    g = N_HEADS
    gw = g * HEAD_DIM
    qb0, kb0, vb0, z0 = 2, 3, 4, 5
    prep_blk = lambda b, s: b * ns + jnp.minimum(s, ns - 1)
    fin_blk = lambda b, s: b * ns + jnp.maximum(s - 1, 0)

    def tok(off):
        return pl.BlockSpec((r, gw), lambda b, s: (prep_blk(b, s), off))

    return pl.pallas_call(
        _gdn_kernel,
        grid=(batch, ns + 1),
        in_specs=[
            tok(qb0), tok(kb0), tok(vb0),
            pl.BlockSpec((r, gw), lambda b, s: (fin_blk(b, s), z0)),
            pl.BlockSpec((r, HEAD_DIM), lambda b, s: (prep_blk(b, s), 0)),
            pl.BlockSpec((2 * g, r), lambda b, s: (0, prep_blk(b, s))),
            pl.BlockSpec((3, CONV_WIDTH, gw), lambda b, s: (0, 0, 0)),
            pl.BlockSpec((1, HEAD_DIM), lambda b, s: (0, 0)),
        ],
        out_specs=pl.BlockSpec((r, gw), lambda b, s: (fin_blk(b, s), 0)),
        out_shape=jax.ShapeDtypeStruct((t, gw), BF16),
        scratch_shapes=[pltpu.VMEM((3, 8, gw), F32),
                        pltpu.VMEM((g, HEAD_DIM, HEAD_DIM), F32),
                        pltpu.VMEM((2, g, r, r), BF16),
                        pltpu.VMEM((2, g, r, r), BF16),
                        pltpu.VMEM((2, g, r, 2 * HEAD_DIM), F32),
                        pltpu.VMEM((2, g, r, HEAD_DIM), F32),
                        pltpu.VMEM((2, g, r, HEAD_DIM), BF16),
                        pltpu.VMEM((2, r, HEAD_DIM), F32)],
        compiler_params=pltpu.CompilerParams(
            dimension_semantics=("arbitrary", "arbitrary"),
            vmem_limit_bytes=VMEM_LIMIT),
        name="gdn",
    )(proj, proj, proj, proj, colf, rowf, conv_w3, gdn_norm)


def _merge_kernel(ya_ref, yb_ref, ga_ref, gb_ref, x_ref, wa_ref, wb_ref, wo_ref, ln_ref, o_ref):
    merged = (_sigmoid(ga_ref[...].astype(F32)) * _dot(ya_ref[...], wa_ref[...])
              + _sigmoid(gb_ref[...].astype(F32)) * _dot(yb_ref[...], wb_ref[...]))
    out = _dot(merged.astype(BF16), wo_ref[...])
    o_ref[...] = x_ref[...] + _rms(out, ln_ref[...])


def _merge(ya, yb, proj, x2, wa, wb, wo, ln, tm):
    t, d = x2.shape
    gate0 = 6
    const = lambda i: (0, 0)
    return pl.pallas_call(
        _merge_kernel,
        grid=(t // tm,),
        in_specs=[
            pl.BlockSpec((tm, d), lambda i: (i, 0)),
            pl.BlockSpec((tm, d), lambda i: (i, 0)),
            pl.BlockSpec((tm, d), lambda i: (i, gate0)),
            pl.BlockSpec((tm, d), lambda i: (i, gate0 + 1)),
            pl.BlockSpec((tm, d), lambda i: (i, 0)),
            pl.BlockSpec((d, d), const), pl.BlockSpec((d, d), const), pl.BlockSpec((d, d), const),
            pl.BlockSpec((1, d), const),
        ],
        out_specs=pl.BlockSpec((tm, d), lambda i: (i, 0)),
        out_shape=jax.ShapeDtypeStruct((t, d), F32),
        compiler_params=pltpu.CompilerParams(
            dimension_semantics=("arbitrary",), vmem_limit_bytes=VMEM_LIMIT),
        name="merge",
    )(ya, yb, proj, proj, x2, wa, wb, wo, ln)


def _ffn_ple_kernel(h_ref, p_ref, ln1_ref, wg_ref, wu_ref, wd_ref, ln2_ref,
                    wp_ref, lnp_ref, wpg_ref, o_ref):
    h1 = h_ref[...]
    f = _rms(h1, ln1_ref[...]).astype(BF16)
    act = (_silu(_dot(f, wg_ref[...])) * _dot(f, wu_ref[...])).astype(BF16)
    h2 = h1 + _rms(_dot(act, wd_ref[...]), ln2_ref[...])
    e = _rms(_dot(p_ref[...].astype(BF16), wp_ref[...]), lnp_ref[...])
    o_ref[...] = h2 + _sigmoid(_dot(h2.astype(BF16), wpg_ref[...])) * e


def _ffn_ple(h1, p2, ln1, wg, wu, wd, ln2, wp, lnp, wpg, tm):
    t, d = h1.shape
    dff = wg.shape[1]
    dp = p2.shape[1]
    const = lambda i: (0, 0)
    once = pl.Buffered(1)

    def resident(shape):
        return pl.BlockSpec(shape, const, pipeline_mode=once)

    return pl.pallas_call(
        _ffn_ple_kernel,
        grid=(t // tm,),
        in_specs=[
            pl.BlockSpec((tm, d), lambda i: (i, 0)),
            pl.BlockSpec((tm, dp), lambda i: (i, 0)),
            resident((1, d)), resident((d, dff)), resident((d, dff)), resident((dff, d)),
            resident((1, d)), resident((dp, d)), resident((1, d)), resident((d, d)),
        ],
        out_specs=pl.BlockSpec((tm, d), lambda i: (i, 0)),
        out_shape=jax.ShapeDtypeStruct((t, d), F32),
        compiler_params=pltpu.CompilerParams(
            dimension_semantics=("arbitrary",), vmem_limit_bytes=VMEM_LIMIT),
        name="ffn_ple",
    )(h1, p2, ln1, wg, wu, wd, ln2, wp, lnp, wpg)


def _rope_tables(seq):
    inv = 1.0 / (ROPE_THETA ** (jnp.arange(0, HEAD_DIM, 2, dtype=F32) / HEAD_DIM))
    ang = jnp.arange(seq, dtype=F32)[:, None] * inv[None, :]
    cos, sin = jnp.cos(ang), jnp.sin(ang)
    return jnp.concatenate([cos, cos], axis=1), jnp.concatenate([-sin, sin], axis=1)


def _layer(h, p_i, ln_pre_mix, w_in, conv_w, a_log, dt_bias, gdn_norm, w_proj_a, w_proj_b,
           w_out, ln_post_mix, ln_pre_ffn, w_ffn_gate, w_ffn_up, w_ffn_down, ln_post_ffn,
           w_ple, ln_ple, w_ple_gate):
    b, s, d = h.shape
    t = b * s
    width = N_HEADS * HEAD_DIM
    assert d == width and s % MOBA_BLOCK == 0 and s % GDN_ROWS == 0
    x2 = h.reshape(t, d)
    row = lambda v: v.reshape(1, -1).astype(F32)

    c_small = 7 * width
    c_gate = c_small + 2 * N_HEADS
    w_main = jnp.concatenate(
        [w_in[:, :2 * width], w_in[:, 3 * width:c_small], w_in[:, c_gate:]], axis=1).astype(BF16)
    w_vt = w_in[:, 2 * width:3 * width].T.astype(BF16)
    w_small = jnp.pad(w_in[:, c_small:c_gate], ((0, 0), (0, HEAD_DIM - 2 * N_HEADS))).astype(BF16)
    cosf, sinf = _rope_tables(s)
    tm_in = min(1024, s)
    proj, vt, ab, abt = _inproj(x2, row(ln_pre_mix), w_main, w_vt, w_small, w_small.T,
                                 cosf, sinf, s, tm_in)

    key_block = jnp.arange(s, dtype=jnp.int32)[:, None] // MOBA_BLOCK
    onehot = (key_block == jnp.arange(HEAD_DIM, dtype=jnp.int32)[None, :]).astype(BF16)
    y_a = _moba(proj, vt, onehot, b, s)

    decay_rows = lambda v: jnp.pad(v.astype(F32), (N_HEADS, 0)).reshape(-1, 1)
    colf, rowf = _gdn_gates(ab, abt, decay_rows(a_log), decay_rows(dt_bias))
    conv_w3 = conv_w.astype(F32).reshape(CONV_WIDTH, 3, width).transpose(1, 0, 2)
    y_b = _gdn(proj, colf, rowf, conv_w3, row(gdn_norm), b, s)

    tm = min(512, t)
    h1 = _merge(y_a, y_b, proj, x2, w_proj_a.astype(BF16), w_proj_b.astype(BF16),
                w_out.astype(BF16), row(ln_post_mix), tm)
    out = _ffn_ple(h1, p_i.reshape(t, -1), row(ln_pre_ffn), w_ffn_gate.astype(BF16),
                   w_ffn_up.astype(BF16), w_ffn_down.astype(BF16), row(ln_post_ffn),
                   w_ple.astype(BF16), row(ln_ple), w_ple_gate.astype(BF16), tm)
    return out.reshape(b, s, d)


def kernel(x, p, ln_pre_mix, w_in, conv_w, a_log, dt_bias, gdn_norm, w_proj_a, w_proj_b, w_out,
           ln_post_mix, ln_pre_ffn, w_ffn_gate, w_ffn_up, w_ffn_down, ln_post_ffn, w_ple, ln_ple,
           w_ple_gate):
    h = x
    for i in range(p.shape[0]):
        h = _layer(h, p[i], ln_pre_mix[i], w_in[i], conv_w[i], a_log[i], dt_bias[i], gdn_norm[i],
                   w_proj_a[i], w_proj_b[i], w_out[i], ln_post_mix[i], ln_pre_ffn[i],
                   w_ffn_gate[i], w_ffn_up[i], w_ffn_down[i], ln_post_ffn[i], w_ple[i],
                   ln_ple[i], w_ple_gate[i])
    return h
```

```python
import jax
import jax.numpy as jnp
from jax import lax
from jax.experimental import pallas as pl
from jax.experimental.pallas import tpu as pltpu

F32 = jnp.float32
BF16 = jnp.bfloat16

HEAD_DIM = 128
N_HEADS = 8
INPROJ_CHUNK = 256
MOBA_BLOCK = 256
MOBA_TOPK = 3
MOBA_HEADS = 4
MOBA_PAST_STEP = 2
GDN_CHUNK = 64
GDN_ROWS = 256
CONV_WIDTH = 4
EPS = 1e-6
ROPE_THETA = 10000.0
NEG_BIG = -1e30
LOG2_E = 1.4426950408889634
VMEM_LIMIT = 56 * 1024 * 1024

_NT = (((1,), (1,)), ((), ()))
_TN = (((0,), (0,)), ((), ()))


def _dot(a, b):
    return jnp.dot(a, b, preferred_element_type=F32)


def _dot_nt(a, b):
    return lax.dot_general(a, b, _NT, preferred_element_type=F32)


def _dot_tn(a, b):
    return lax.dot_general(a, b, _TN, preferred_element_type=F32)


def _rms(x, w):
    return x * lax.rsqrt(jnp.mean(x * x, axis=-1, keepdims=True) + EPS) * w


def _sigmoid(x):
    return 1.0 / (1.0 + jnp.exp(-x))


def _silu(x):
    return x * _sigmoid(x)


def _softplus(x):
    return jnp.maximum(x, 0.0) + jnp.log1p(jnp.exp(-jnp.abs(x)))


def _inproj_kernel(x_ref, ln_ref, w_ref, wvt_ref, ws_ref, wst_ref, cos_ref, sin_ref,
                   o_ref, vt_ref, ab_ref, abt_ref, u_ref):
    j = pl.program_id(1)

    @pl.when(j == 0)
    def _():
        u = _rms(x_ref[...], ln_ref[...]).astype(BF16)
        u_ref[...] = u
        ab_ref[...] = _dot(u, ws_ref[...])
        abt_ref[...] = _dot_nt(wst_ref[...], u)
        for c in range(vt_ref.shape[0] // INPROJ_CHUNK):
            rows = slice(c * INPROJ_CHUNK, (c + 1) * INPROJ_CHUNK)
            vt_ref[rows, :] = _dot_nt(wvt_ref[rows, :], u).astype(BF16)

    n_chunks = o_ref.shape[1] // INPROJ_CHUNK
    chunk = lambda c: slice(c * INPROJ_CHUNK, (c + 1) * INPROJ_CHUNK)

    @pl.when(j < 2)
    def _():
        u = u_ref[...]
        qscale = jnp.where(j == 0, LOG2_E * HEAD_DIM ** -0.5, 1.0).astype(F32)
        cosf = cos_ref[...] * qscale
        sinf = sin_ref[...] * qscale
        for c in range(n_chunks):
            acc = _dot(u, w_ref[:, chunk(c)])
            for h in range(INPROJ_CHUNK // HEAD_DIM):
                seg = acc[:, h * HEAD_DIM:(h + 1) * HEAD_DIM]
                rot = seg * cosf + pltpu.roll(seg, HEAD_DIM // 2, axis=1) * sinf
                lo = c * INPROJ_CHUNK + h * HEAD_DIM
                o_ref[:, lo:lo + HEAD_DIM] = rot.astype(BF16)

    @pl.when(j >= 2)
    def _():
        u = u_ref[...]
        for c in range(n_chunks):
            o_ref[:, chunk(c)] = _dot(u, w_ref[:, chunk(c)]).astype(BF16)


def _inproj(x2, ln, w_main, w_vt, w_small, w_small_t, cosf, sinf, seq, tm):
    t, d = x2.shape
    n_main = w_main.shape[1]
    n_small = w_small.shape[1]
    tn = N_HEADS * HEAD_DIM
    n_seq_tiles = seq // tm
    return pl.pallas_call(
        _inproj_kernel,
        grid=(t // tm, n_main // tn),
        in_specs=[
            pl.BlockSpec((tm, d), lambda i, j: (i, 0)),
            pl.BlockSpec((1, d), lambda i, j: (0, 0)),
            pl.BlockSpec((d, tn), lambda i, j: (0, j)),
            pl.BlockSpec((tn, d), lambda i, j: (0, 0)),
            pl.BlockSpec((d, n_small), lambda i, j: (0, 0)),
            pl.BlockSpec((n_small, d), lambda i, j: (0, 0)),
            pl.BlockSpec((tm, HEAD_DIM), lambda i, j: (i % n_seq_tiles, 0)),
            pl.BlockSpec((tm, HEAD_DIM), lambda i, j: (i % n_seq_tiles, 0)),
        ],
        out_specs=[
            pl.BlockSpec((tm, tn), lambda i, j: (i, j)),
            pl.BlockSpec((tn, tm), lambda i, j: (0, i)),
            pl.BlockSpec((tm, n_small), lambda i, j: (i, 0)),
            pl.BlockSpec((n_small, tm), lambda i, j: (0, i)),
        ],
        out_shape=[
            jax.ShapeDtypeStruct((t, n_main), BF16),
            jax.ShapeDtypeStruct((tn, t), BF16),
            jax.ShapeDtypeStruct((t, n_small), F32),
            jax.ShapeDtypeStruct((n_small, t), F32),
        ],
        scratch_shapes=[pltpu.VMEM((tm, d), BF16)],
        compiler_params=pltpu.CompilerParams(
            dimension_semantics=("arbitrary", "arbitrary"), vmem_limit_bytes=VMEM_LIMIT),
        name="inproj",
    )(x2, ln, w_main, w_vt, w_small, w_small_t, cosf, sinf)


def _moba_kernel(q_ref, kd_ref, vtd_ref, k_ref, vt_ref, o_ref, kmh_ref, kml_ref, s_ref):
    i = pl.program_id(2)
    blk = MOBA_BLOCK
    n_blocks = k_ref.shape[0] // blk
    heads = q_ref.shape[1] // HEAD_DIM
    hsl = lambda g: slice(g * HEAD_DIM, (g + 1) * HEAD_DIM)

    @pl.when(i == 0)
    def _():
        for g in range(heads):
            km = jnp.mean(k_ref[:, hsl(g)].astype(F32).reshape(n_blocks, blk, HEAD_DIM), axis=1)
            hi = km.astype(BF16)
            kmh_ref[g] = hi
            kml_ref[g] = (km - hi.astype(F32)).astype(BF16)

    def attend(n_past):
        sub = lax.broadcasted_iota(jnp.int32, (n_blocks, blk), 0)
        sub_f = sub.astype(F32)
        past = sub < i
        krow = lax.broadcasted_iota(jnp.int32, (blk, blk), 0)
        qcol = lax.broadcasted_iota(jnp.int32, (blk, blk), 1)
        causal = krow <= qcol
        nk = n_past * blk
        halves = ((0, nk // 2), (nk // 2, nk)) if n_past else ()

        def scores(g):
            q = q_ref[:, hsl(g)]
            s_d = jnp.where(causal, _dot_nt(kd_ref[:, hsl(g)], q), -jnp.inf)
            m = jnp.max(s_d, axis=0, keepdims=True)
            if n_past:
                gate = _dot_nt(kmh_ref[g], q) + _dot_nt(kml_ref[g], q)
                gsel = jnp.where(past, gate, -jnp.inf)
                sel = jnp.zeros((n_blocks, blk), jnp.bool_)
                for _ in range(MOBA_TOPK):
                    mx = jnp.max(gsel, axis=0, keepdims=True)
                    idx = jnp.min(jnp.where(gsel == mx, sub_f, float(n_blocks)),
                                  axis=0, keepdims=True)
                    hit = sub_f == idx
                    sel = jnp.logical_or(sel, hit)
                    gsel = jnp.where(hit, -jnp.inf, gsel)
                bias = jnp.where(jnp.logical_and(sel, past), 0.0, NEG_BIG)
                for lo, hi in halves:
                    s_ref[g, lo:hi, :] = _dot_nt(k_ref[lo:hi, hsl(g)], q)
                for c in range(n_past):
                    m_c = jnp.max(s_ref[g, c * blk:(c + 1) * blk, :], axis=0, keepdims=True)
                    m = jnp.maximum(m, m_c + bias[c:c + 1, :])
            else:
                bias = None
            return s_d, m, bias

        def weigh(g, s_d, m, bias):
            p_d = jnp.exp2(s_d - m)
            l = jnp.sum(p_d, axis=0, keepdims=True)
            acc = _dot(vtd_ref[hsl(g), :], p_d.astype(BF16))
            for lo, hi in halves:
                p = jnp.concatenate(
                    [jnp.exp2(s_ref[g, c * blk:(c + 1) * blk, :] + (bias[c:c + 1, :] - m))
                     for c in range(lo // blk, hi // blk)], axis=0)
                l = l + jnp.sum(p, axis=0, keepdims=True)
                acc = acc + _dot(vt_ref[hsl(g), lo:hi], p.astype(BF16))
            o_ref[:, hsl(g)] = (acc * (1.0 / l)).T.astype(BF16)

        pending = scores(0)
        for g in range(heads):
            upcoming = scores(g + 1) if g + 1 < heads else None
            weigh(g, *pending)
            pending = upcoming

    step = MOBA_PAST_STEP
    for n_past in range(0, n_blocks + step, step):
        n_past = min(n_past, n_blocks)
        lo = n_past - step + 1 if n_past else 0
        pl.when(jnp.logical_and(i >= lo, i <= n_past))(lambda n=n_past: attend(n))
        if n_past == n_blocks:
            break


def _moba(proj, vt, batch, seq):
    t = proj.shape[0]
    nb = seq // MOBA_BLOCK
    g = MOBA_HEADS
    gw = g * HEAD_DIM
    k0 = N_HEADS // g
    return pl.pallas_call(
        _moba_kernel,
        grid=(batch, N_HEADS // g, nb),
        in_specs=[
            pl.BlockSpec((MOBA_BLOCK, gw), lambda b, h, i: (b * nb + i, h)),
            pl.BlockSpec((MOBA_BLOCK, gw), lambda b, h, i: (b * nb + i, k0 + h)),
            pl.BlockSpec((gw, MOBA_BLOCK), lambda b, h, i: (h, b * nb + i)),
            pl.BlockSpec((seq, gw), lambda b, h, i: (b, k0 + h)),
            pl.BlockSpec((gw, seq), lambda b, h, i: (h, b)),
        ],
        out_specs=pl.BlockSpec((MOBA_BLOCK, gw), lambda b, h, i: (b * nb + i, h)),
        out_shape=jax.ShapeDtypeStruct((t, N_HEADS * HEAD_DIM), BF16),
        scratch_shapes=[pltpu.VMEM((g, nb, HEAD_DIM), BF16),
                        pltpu.VMEM((g, nb, HEAD_DIM), BF16),
                        pltpu.VMEM((g, seq, MOBA_BLOCK), F32)],
        compiler_params=pltpu.CompilerParams(
            dimension_semantics=("arbitrary", "arbitrary", "arbitrary"),
            vmem_limit_bytes=VMEM_LIMIT),
        name="moba",
    )(proj, proj, vt, proj, vt)


def _chunk_masks(n):
    row = lax.broadcasted_iota(jnp.int32, (n, n), 0)
    col = lax.broadcasted_iota(jnp.int32, (n, n), 1)
    shift = GDN_CHUNK.bit_length() - 1
    same = (row >> shift) == (col >> shift)
    return row, col, same


def _split3(x):
    hi = x.astype(BF16)
    rest = x - hi.astype(F32)
    mid = rest.astype(BF16)
    lo = (rest - mid.astype(F32)).astype(BF16)
    return hi, mid, lo


def _gdn_gates_kernel(ab_ref, abt_ref, alog_col_ref, dt_col_ref, col_ref, rowf_ref):
    r = ab_ref.shape[0]
    nh = N_HEADS
    row, col, same = _chunk_masks(r)
    triu = jnp.where(jnp.logical_and(same, row <= col), 1.0, 0.0).astype(BF16)
    ones = jnp.where(same, 1.0, 0.0).astype(BF16)

    abt = abt_ref[...]
    srow = lax.broadcasted_iota(jnp.int32, abt.shape, 0)
    gt = -jnp.exp(alog_col_ref[...]) * _softplus(abt + dt_col_ref[...])
    gt = jnp.where(srow >= nh, gt, 0.0)
    pieces = _split3(gt)
    gc_t = sum(_dot(p, triu) for p in pieces)
    gl_t = sum(_dot(p, ones) for p in pieces)
    rowf_ref[...] = gc_t

    stack = jnp.concatenate(
        [gc_t[nh:2 * nh, :], gl_t[nh:2 * nh, :], jnp.zeros((HEAD_DIM - 2 * nh, r), F32)], axis=0)
    shifted = pltpu.roll(stack.T, nh, axis=1)
    ab = ab_ref[...]
    lane = lax.broadcasted_iota(jnp.int32, ab.shape, 1)
    col_ref[...] = jnp.where(lane < nh, _sigmoid(ab), shifted)


def _gdn_gates(ab, abt, alog_col, dt_col):
    t = ab.shape[0]
    r = GDN_ROWS
    rows = 2 * N_HEADS
    return pl.pallas_call(
        _gdn_gates_kernel,
        grid=(t // r,),
        in_specs=[
            pl.BlockSpec((r, HEAD_DIM), lambda i: (i, 0)),
            pl.BlockSpec((rows, r), lambda i: (0, i)),
            pl.BlockSpec((rows, 1), lambda i: (0, 0)),
            pl.BlockSpec((rows, 1), lambda i: (0, 0)),
        ],
        out_specs=[
            pl.BlockSpec((r, HEAD_DIM), lambda i: (i, 0)),
            pl.BlockSpec((rows, r), lambda i: (0, i)),
        ],
        out_shape=[
            jax.ShapeDtypeStruct((t, HEAD_DIM), F32),
            jax.ShapeDtypeStruct((rows, t), F32),
        ],
        compiler_params=pltpu.CompilerParams(dimension_semantics=("arbitrary",)),
        name="gdn_gates",
    )(ab, abt, alog_col, dt_col)


def _gdn_kernel(q_ref, k_ref, v_ref, z_ref, colf_ref, rowf_ref, cw_ref, gn_ref,
                o_ref, xe_ref, state_ref, cp_ref, at_ref, x_ref, qe_ref, kd_ref, colp_ref):
    sblk = pl.program_id(1)
    r = q_ref.shape[0]
    heads = q_ref.shape[1] // HEAD_DIM
    n_chunks = r // GDN_CHUNK
    n_levels = (GDN_CHUNK - 1).bit_length()
    pad = 8
    hsl = lambda g: slice(g * HEAD_DIM, (g + 1) * HEAD_DIM)
    handover = (cp_ref, at_ref, x_ref, qe_ref, kd_ref, colp_ref)

    @pl.when(jnp.logical_and(pl.program_id(0) == 0, sblk == 0))
    def _():
        for ref in handover:
            ref[...] = jnp.zeros_like(ref)
        state_ref[...] = jnp.zeros_like(state_ref)

    @pl.when(sblk == 0)
    def _():
        xe_ref[...] = jnp.zeros_like(xe_ref)

    def prepare(slot):
        def conv_silu(c, raw_ref):
            x = raw_ref[...].astype(F32)
            xe = jnp.concatenate([xe_ref[c], x], axis=0)
            w = cw_ref[c]
            y = x * w[CONV_WIDTH - 1:CONV_WIDTH, :]
            for d in range(1, CONV_WIDTH):
                y = y + pltpu.roll(xe, d, axis=0)[pad:, :] * w[CONV_WIDTH - 1 - d:CONV_WIDTH - d, :]
            xe_ref[c] = x[r - pad:r, :]
            return _silu(y)

        q_all = conv_silu(0, q_ref)
        k_all = conv_silu(1, k_ref)
        v_all = conv_silu(2, v_ref)

        row, col, same = _chunk_masks(r)
        causal = jnp.logical_and(same, col <= row)
        strict = jnp.logical_and(same, col < row)
        colf = colf_ref[...]
        colp_ref[slot] = colf
        for g in range(heads):
            q = q_all[:, hsl(g)]
            k = k_all[:, hsl(g)]
            v = v_all[:, hsl(g)]
            q = q * lax.rsqrt(jnp.sum(q * q, axis=-1, keepdims=True) + EPS) * (HEAD_DIM ** -0.5)
            k = k * lax.rsqrt(jnp.sum(k * k, axis=-1, keepdims=True) + EPS)

            beta = colf[:, g:g + 1]
            gc = colf[:, heads + g:heads + g + 1]
            glast = colf[:, 2 * heads + g:2 * heads + g + 1]
            gc_row = rowf_ref[heads + g:heads + g + 1, :]
            decay = jnp.exp(jnp.where(causal, gc - gc_row, -jnp.inf))

            kb = k * beta
            k16 = k.astype(BF16)
            cp_ref[slot, g] = jnp.where(
                strict, -_dot_nt(kb.astype(BF16), k16) * decay, 0.0).astype(BF16)
            at_ref[slot, g] = (_dot_nt(q.astype(BF16), k16) * decay).astype(BF16)
            egc = jnp.exp(gc)
            x_ref[slot, g] = jnp.concatenate([v * beta, kb * egc], axis=1)
            qe_ref[slot, g] = q * egc
            kd_ref[slot, g] = (k * jnp.exp(glast - gc)).astype(BF16)

    def finish(slot):
        hd_range = range(heads)
        c_pow = [cp_ref[slot, g] for g in hd_range]
        x = [x_ref[slot, g] for g in hd_range]
        for level in range(n_levels):
            for g in hd_range:
                x[g] = x[g] + _dot(c_pow[g], x[g].astype(BF16))
                if level + 1 < n_levels:
                    c_pow[g] = _dot(c_pow[g], c_pow[g]).astype(BF16)

        p_mat, r_mat, nu_g = [], [], []
        for g in hd_range:
            sol16 = x[g].astype(BF16)
            aw_au = _dot(at_ref[slot, g], sol16)
            p_mat.append((qe_ref[slot, g] - aw_au[:, HEAD_DIM:]).astype(BF16))
            r_mat.append(aw_au[:, :HEAD_DIM])
            kd = kd_ref[slot, g]
            nu_g.append([_dot_tn(kd[c * GDN_CHUNK:(c + 1) * GDN_CHUNK, :],
                                 sol16[c * GDN_CHUNK:(c + 1) * GDN_CHUNK, :])
                         for c in range(n_chunks)])

        colp = colp_ref[slot]
        states = [state_ref[g] for g in hd_range]
        outs = [[] for _ in hd_range]
        for c in range(n_chunks):
            lo, hi = c * GDN_CHUNK, (c + 1) * GDN_CHUNK
            for g in hd_range:
                s16 = states[g].astype(BF16)
                e_c = jnp.exp(colp[lo:lo + 1, 2 * heads + g:2 * heads + g + 1])
                outs[g].append(_dot(p_mat[g][lo:hi, :], s16) + r_mat[g][lo:hi, :])
                states[g] = (e_c * states[g] - _dot(nu_g[g][c][:, HEAD_DIM:].astype(BF16), s16)
                             + nu_g[g][c][:, :HEAD_DIM])

        gnorm = gn_ref[...]
        for g in hd_range:
            state_ref[g] = jnp.where(sblk > 0, states[g], 0.0)
            o = jnp.concatenate(outs[g], axis=0)
            z = z_ref[:, hsl(g)].astype(F32)
            o_ref[:, hsl(g)] = (_rms(o, gnorm) * _silu(z)).astype(BF16)

    for parity in range(2):
        @pl.when(sblk % 2 == parity)
        def _():
            finish(1 - parity)
            prepare(parity)


def _gdn(proj, colf, rowf, conv_w3, gdn_norm, batch, seq):
    t = proj.shape[0]
    r = GDN_ROWS
    ns = seq // r
    assert ns % 2 == 0
    g = N_HEADS
    gw = g * HEAD_DIM
    qb0, kb0, vb0, z0 = 2, 3, 4, 5
    prep_blk = lambda b, s: b * ns + jnp.minimum(s, ns - 1)
    fin_blk = lambda b, s: b * ns + jnp.maximum(s - 1, 0)

    def tok(off):
        return pl.BlockSpec((r, gw), lambda b, s: (prep_blk(b, s), off))

    return pl.pallas_call(
        _gdn_kernel,
        grid=(batch, ns + 1),
        in_specs=[
            tok(qb0), tok(kb0), tok(vb0),
            pl.BlockSpec((r, gw), lambda b, s: (fin_blk(b, s), z0)),
            pl.BlockSpec((r, HEAD_DIM), lambda b, s: (prep_blk(b, s), 0)),
            pl.BlockSpec((2 * g, r), lambda b, s: (0, prep_blk(b, s))),
            pl.BlockSpec((3, CONV_WIDTH, gw), lambda b, s: (0, 0, 0)),
            pl.BlockSpec((1, HEAD_DIM), lambda b, s: (0, 0)),
        ],
        out_specs=pl.BlockSpec((r, gw), lambda b, s: (fin_blk(b, s), 0)),
        out_shape=jax.ShapeDtypeStruct((t, gw), BF16),
        scratch_shapes=[pltpu.VMEM((3, 8, gw), F32),
                        pltpu.VMEM((g, HEAD_DIM, HEAD_DIM), F32),
                        pltpu.VMEM((2, g, r, r), BF16),
                        pltpu.VMEM((2, g, r, r), BF16),
                        pltpu.VMEM((2, g, r, 2 * HEAD_DIM), F32),
                        pltpu.VMEM((2, g, r, HEAD_DIM), F32),
                        pltpu.VMEM((2, g, r, HEAD_DIM), BF16),
                        pltpu.VMEM((2, r, HEAD_DIM), F32)],
        compiler_params=pltpu.CompilerParams(
            dimension_semantics=("arbitrary", "arbitrary"),
            vmem_limit_bytes=VMEM_LIMIT),
        name="gdn",
    )(proj, proj, proj, proj, colf, rowf, conv_w3, gdn_norm)


def _merge_kernel(ya_ref, yb_ref, ga_ref, gb_ref, x_ref, wa_ref, wb_ref, wo_ref, ln_ref, o_ref):
    merged = (_sigmoid(ga_ref[...].astype(F32)) * _dot(ya_ref[...], wa_ref[...])
              + _sigmoid(gb_ref[...].astype(F32)) * _dot(yb_ref[...], wb_ref[...]))
    out = _dot(merged.astype(BF16), wo_ref[...])
    o_ref[...] = x_ref[...] + _rms(out, ln_ref[...])


def _merge(ya, yb, proj, x2, wa, wb, wo, ln, tm):
    t, d = x2.shape
    gate0 = 6
    const = lambda i: (0, 0)
    return pl.pallas_call(
        _merge_kernel,
        grid=(t // tm,),
        in_specs=[
            pl.BlockSpec((tm, d), lambda i: (i, 0)),
            pl.BlockSpec((tm, d), lambda i: (i, 0)),
            pl.BlockSpec((tm, d), lambda i: (i, gate0)),
            pl.BlockSpec((tm, d), lambda i: (i, gate0 + 1)),
            pl.BlockSpec((tm, d), lambda i: (i, 0)),
            pl.BlockSpec((d, d), const), pl.BlockSpec((d, d), const), pl.BlockSpec((d, d), const),
            pl.BlockSpec((1, d), const),
        ],
        out_specs=pl.BlockSpec((tm, d), lambda i: (i, 0)),
        out_shape=jax.ShapeDtypeStruct((t, d), F32),
        compiler_params=pltpu.CompilerParams(
            dimension_semantics=("arbitrary",), vmem_limit_bytes=VMEM_LIMIT),
        name="merge",
    )(ya, yb, proj, proj, x2, wa, wb, wo, ln)


def _ffn_ple_kernel(h_ref, p_ref, ln1_ref, wg_ref, wu_ref, wd_ref, ln2_ref,
                    wp_ref, lnp_ref, wpg_ref, o_ref):
    h1 = h_ref[...]
    f = _rms(h1, ln1_ref[...]).astype(BF16)
    act = (_silu(_dot(f, wg_ref[...])) * _dot(f, wu_ref[...])).astype(BF16)
    h2 = h1 + _rms(_dot(act, wd_ref[...]), ln2_ref[...])
    e = _rms(_dot(p_ref[...].astype(BF16), wp_ref[...]), lnp_ref[...])
    o_ref[...] = h2 + _sigmoid(_dot(h2.astype(BF16), wpg_ref[...])) * e


def _ffn_ple(h1, p2, ln1, wg, wu, wd, ln2, wp, lnp, wpg, tm):
    t, d = h1.shape
    dff = wg.shape[1]
    dp = p2.shape[1]
    const = lambda i: (0, 0)
    once = pl.Buffered(1)

    def resident(shape):
        return pl.BlockSpec(shape, const, pipeline_mode=once)

    return pl.pallas_call(
        _ffn_ple_kernel,
        grid=(t // tm,),
        in_specs=[
            pl.BlockSpec((tm, d), lambda i: (i, 0)),
            pl.BlockSpec((tm, dp), lambda i: (i, 0)),
            resident((1, d)), resident((d, dff)), resident((d, dff)), resident((dff, d)),
            resident((1, d)), resident((dp, d)), resident((1, d)), resident((d, d)),
        ],
        out_specs=pl.BlockSpec((tm, d), lambda i: (i, 0)),
        out_shape=jax.ShapeDtypeStruct((t, d), F32),
        compiler_params=pltpu.CompilerParams(
            dimension_semantics=("arbitrary",), vmem_limit_bytes=VMEM_LIMIT),
        name="ffn_ple",
    )(h1, p2, ln1, wg, wu, wd, ln2, wp, lnp, wpg)


def _rope_tables(seq):
    inv = 1.0 / (ROPE_THETA ** (jnp.arange(0, HEAD_DIM, 2, dtype=F32) / HEAD_DIM))
    ang = jnp.arange(seq, dtype=F32)[:, None] * inv[None, :]
    cos, sin = jnp.cos(ang), jnp.sin(ang)
    return jnp.concatenate([cos, cos], axis=1), jnp.concatenate([-sin, sin], axis=1)


def _layer(h, p_i, ln_pre_mix, w_in, conv_w, a_log, dt_bias, gdn_norm, w_proj_a, w_proj_b,
           w_out, ln_post_mix, ln_pre_ffn, w_ffn_gate, w_ffn_up, w_ffn_down, ln_post_ffn,
           w_ple, ln_ple, w_ple_gate):
    b, s, d = h.shape
    t = b * s
    width = N_HEADS * HEAD_DIM
    assert d == width and s % MOBA_BLOCK == 0 and s % GDN_ROWS == 0
    x2 = h.reshape(t, d)
    row = lambda v: v.reshape(1, -1).astype(F32)

    c_small = 7 * width
    c_gate = c_small + 2 * N_HEADS
    w_main = jnp.concatenate(
        [w_in[:, :2 * width], w_in[:, 3 * width:c_small], w_in[:, c_gate:]], axis=1).astype(BF16)
    w_vt = w_in[:, 2 * width:3 * width].T.astype(BF16)
    w_small = jnp.pad(w_in[:, c_small:c_gate], ((0, 0), (0, HEAD_DIM - 2 * N_HEADS))).astype(BF16)
    cosf, sinf = _rope_tables(s)
    tm_in = min(1024, s)
    proj, vt, ab, abt = _inproj(x2, row(ln_pre_mix), w_main, w_vt, w_small, w_small.T,
                                 cosf, sinf, s, tm_in)

    y_a = _moba(proj, vt, b, s)

    decay_rows = lambda v: jnp.pad(v.astype(F32), (N_HEADS, 0)).reshape(-1, 1)
    colf, rowf = _gdn_gates(ab, abt, decay_rows(a_log), decay_rows(dt_bias))
    conv_w3 = conv_w.astype(F32).reshape(CONV_WIDTH, 3, width).transpose(1, 0, 2)
    y_b = _gdn(proj, colf, rowf, conv_w3, row(gdn_norm), b, s)

    tm = min(512, t)
    h1 = _merge(y_a, y_b, proj, x2, w_proj_a.astype(BF16), w_proj_b.astype(BF16),
                w_out.astype(BF16), row(ln_post_mix), tm)
    out = _ffn_ple(h1, p_i.reshape(t, -1), row(ln_pre_ffn), w_ffn_gate.astype(BF16),
                   w_ffn_up.astype(BF16), w_ffn_down.astype(BF16), row(ln_post_ffn),
                   w_ple.astype(BF16), row(ln_ple), w_ple_gate.astype(BF16), tm)
    return out.reshape(b, s, d)


def kernel(x, p, ln_pre_mix, w_in, conv_w, a_log, dt_bias, gdn_norm, w_proj_a, w_proj_b, w_out,
           ln_post_mix, ln_pre_ffn, w_ffn_gate, w_ffn_up, w_ffn_down, ln_post_ffn, w_ple, ln_ple,
           w_ple_gate):
    h = x
    for i in range(p.shape[0]):
        h = _layer(h, p[i], ln_pre_mix[i], w_in[i], conv_w[i], a_log[i], dt_bias[i], gdn_norm[i],
                   w_proj_a[i], w_proj_b[i], w_out[i], ln_post_mix[i], ln_pre_ffn[i],
                   w_ffn_gate[i], w_ffn_up[i], w_ffn_down[i], ln_post_ffn[i], w_ple[i],
                   ln_ple[i], w_ple_gate[i])
    return h
```

```python
import jax
import jax.numpy as jnp
from jax import lax
from jax.experimental import pallas as pl
from jax.experimental.pallas import tpu as pltpu

F32 = jnp.float32
BF16 = jnp.bfloat16

HEAD_DIM = 128
N_HEADS = 8
INPROJ_CHUNK = 256
MOBA_BLOCK = 256
MOBA_TOPK = 3
MOBA_HEADS = 4
MOBA_PAST_STEP = 2
GDN_CHUNK = 64
GDN_ROWS = 256
CONV_WIDTH = 4
EPS = 1e-6
ROPE_THETA = 10000.0
NEG_BIG = -1e30
LOG2_E = 1.4426950408889634
VMEM_LIMIT = 56 * 1024 * 1024

_NT = (((1,), (1,)), ((), ()))
_TN = (((0,), (0,)), ((), ()))


def _dot(a, b):
    return jnp.dot(a, b, preferred_element_type=F32)


def _dot_nt(a, b):
    return lax.dot_general(a, b, _NT, preferred_element_type=F32)


def _dot_tn(a, b):
    return lax.dot_general(a, b, _TN, preferred_element_type=F32)


def _rms(x, w):
    return x * lax.rsqrt(jnp.mean(x * x, axis=-1, keepdims=True) + EPS) * w


def _sigmoid(x):
    return 1.0 / (1.0 + jnp.exp(-x))


def _silu(x):
    return x * _sigmoid(x)


def _softplus(x):
    return jnp.maximum(x, 0.0) + jnp.log1p(jnp.exp(-jnp.abs(x)))


def _inproj_kernel(x_ref, ln_ref, w_ref, wvt_ref, ws_ref, wst_ref, cos_ref, sin_ref,
                   alog_ref, dt_ref, o_ref, vt_ref, colf_ref, rowf_ref, u_ref):
    j = pl.program_id(1)

    @pl.when(j == 0)
    def _():
        u = _rms(x_ref[...], ln_ref[...]).astype(BF16)
        u_ref[...] = u
        ab = _dot(u, ws_ref[...])
        abt = _dot_nt(wst_ref[...], u)
        for r0 in range(0, u.shape[0], GDN_ROWS):
            colf, rowf = _gdn_gates(ab[r0:r0 + GDN_ROWS, :], abt[:, r0:r0 + GDN_ROWS],
                                    alog_ref[...], dt_ref[...])
            colf_ref[r0:r0 + GDN_ROWS, :] = colf
            rowf_ref[:, r0:r0 + GDN_ROWS] = rowf
        for c in range(vt_ref.shape[0] // INPROJ_CHUNK):
            rows = slice(c * INPROJ_CHUNK, (c + 1) * INPROJ_CHUNK)
            vt_ref[rows, :] = _dot_nt(wvt_ref[rows, :], u).astype(BF16)

    n_chunks = o_ref.shape[1] // INPROJ_CHUNK
    chunk = lambda c: slice(c * INPROJ_CHUNK, (c + 1) * INPROJ_CHUNK)

    @pl.when(j == 0)
    def _():
        u = u_ref[...]
        cos_k, sin_k = cos_ref[...], sin_ref[...]
        qscale = LOG2_E * HEAD_DIM ** -0.5
        cos_q, sin_q = cos_k * qscale, sin_k * qscale
        for c in range(n_chunks):
            cosf, sinf = (cos_q, sin_q) if c < n_chunks // 2 else (cos_k, sin_k)
            acc = _dot(u, w_ref[:, chunk(c)])
            for h in range(INPROJ_CHUNK // HEAD_DIM):
                seg = acc[:, h * HEAD_DIM:(h + 1) * HEAD_DIM]
                rot = seg * cosf + pltpu.roll(seg, HEAD_DIM // 2, axis=1) * sinf
                lo = c * INPROJ_CHUNK + h * HEAD_DIM
                o_ref[:, lo:lo + HEAD_DIM] = rot.astype(BF16)

    @pl.when(j >= 1)
    def _():
        u = u_ref[...]
        for c in range(n_chunks):
            o_ref[:, chunk(c)] = _dot(u, w_ref[:, chunk(c)]).astype(BF16)


def _inproj(x2, ln, w_main, w_vt, w_small, w_small_t, cosf, sinf, alog_col, dt_col, seq, tm):
    t, d = x2.shape
    n_main = w_main.shape[1]
    n_small = w_small.shape[1]
    n_rows = w_small_t.shape[0]
    width = N_HEADS * HEAD_DIM
    tn = 2 * width
    n_seq_tiles = seq // tm
    assert tm % GDN_ROWS == 0 and n_main % tn == 0
    return pl.pallas_call(
        _inproj_kernel,
        grid=(t // tm, n_main // tn),
        in_specs=[
            pl.BlockSpec((tm, d), lambda i, j: (i, 0)),
            pl.BlockSpec((1, d), lambda i, j: (0, 0)),
            pl.BlockSpec((d, tn), lambda i, j: (0, j)),
            pl.BlockSpec((width, d), lambda i, j: (0, 0)),
            pl.BlockSpec((d, n_small), lambda i, j: (0, 0)),
            pl.BlockSpec((n_rows, d), lambda i, j: (0, 0)),
            pl.BlockSpec((tm, HEAD_DIM), lambda i, j: (i % n_seq_tiles, 0)),
            pl.BlockSpec((tm, HEAD_DIM), lambda i, j: (i % n_seq_tiles, 0)),
            pl.BlockSpec((n_rows, 1), lambda i, j: (0, 0)),
            pl.BlockSpec((n_rows, 1), lambda i, j: (0, 0)),
        ],
        out_specs=[
            pl.BlockSpec((tm, tn), lambda i, j: (i, j)),
            pl.BlockSpec((width, tm), lambda i, j: (0, i)),
            pl.BlockSpec((tm, n_small), lambda i, j: (i, 0)),
            pl.BlockSpec((n_rows, tm), lambda i, j: (0, i)),
        ],
        out_shape=[
            jax.ShapeDtypeStruct((t, n_main), BF16),
            jax.ShapeDtypeStruct((width, t), BF16),
            jax.ShapeDtypeStruct((t, n_small), F32),
            jax.ShapeDtypeStruct((n_rows, t), F32),
        ],
        scratch_shapes=[pltpu.VMEM((tm, d), BF16)],
        compiler_params=pltpu.CompilerParams(
            dimension_semantics=("arbitrary", "arbitrary"), vmem_limit_bytes=VMEM_LIMIT),
        name="inproj",
    )(x2, ln, w_main, w_vt, w_small, w_small_t, cosf, sinf, alog_col, dt_col)


def _moba_kernel(q_ref, kd_ref, vtd_ref, k_ref, vt_ref, o_ref, kmh_ref, kml_ref, s_ref):
    i = pl.program_id(2)
    blk = MOBA_BLOCK
    n_blocks = k_ref.shape[0] // blk
    heads = q_ref.shape[1] // HEAD_DIM
    hsl = lambda g: slice(g * HEAD_DIM, (g + 1) * HEAD_DIM)

    @pl.when(i == 0)
    def _():
        for g in range(heads):
            km = jnp.mean(k_ref[:, hsl(g)].astype(F32).reshape(n_blocks, blk, HEAD_DIM), axis=1)
            hi = km.astype(BF16)
            kmh_ref[g] = hi
            kml_ref[g] = (km - hi.astype(F32)).astype(BF16)

    def attend(n_past):
        sub = lax.broadcasted_iota(jnp.int32, (n_blocks, blk), 0)
        sub_f = sub.astype(F32)
        past = sub < i
        krow = lax.broadcasted_iota(jnp.int32, (blk, blk), 0)
        qcol = lax.broadcasted_iota(jnp.int32, (blk, blk), 1)
        causal = krow <= qcol
        nk = n_past * blk
        halves = ((0, nk // 2), (nk // 2, nk)) if n_past else ()

        def scores(g):
            q = q_ref[:, hsl(g)]
            s_d = jnp.where(causal, _dot_nt(kd_ref[:, hsl(g)], q), -jnp.inf)
            m = jnp.max(s_d, axis=0, keepdims=True)
            if n_past:
                gate = _dot_nt(kmh_ref[g], q) + _dot_nt(kml_ref[g], q)
                gsel = jnp.where(past, gate, -jnp.inf)
                sel = jnp.zeros((n_blocks, blk), jnp.bool_)
                for _ in range(MOBA_TOPK):
                    mx = jnp.max(gsel, axis=0, keepdims=True)
                    idx = jnp.min(jnp.where(gsel == mx, sub_f, float(n_blocks)),
                                  axis=0, keepdims=True)
                    hit = sub_f == idx
                    sel = jnp.logical_or(sel, hit)
                    gsel = jnp.where(hit, -jnp.inf, gsel)
                bias = jnp.where(jnp.logical_and(sel, past), 0.0, NEG_BIG)
                for lo, hi in halves:
                    s_ref[g,lo:hi, :] = _dot_nt(k_ref[lo:hi, hsl(g)], q)
                for c in range(n_past):
                    m_c = jnp.max(s_ref[g,c * blk:(c + 1) * blk, :], axis=0, keepdims=True)
                    m = jnp.maximum(m, m_c + bias[c:c + 1, :])
            else:
                bias = None
            return s_d, m, bias

        def weigh(g, s_d, m, bias):
            p_d = jnp.exp2(s_d - m)
            l = jnp.sum(p_d, axis=0, keepdims=True)
            acc = _dot(vtd_ref[hsl(g), :], p_d.astype(BF16))
            for lo, hi in halves:
                p = jnp.concatenate(
                    [jnp.exp2(s_ref[g,c * blk:(c + 1) * blk, :] + (bias[c:c + 1, :] - m))
                     for c in range(lo // blk, hi // blk)], axis=0)
                l = l + jnp.sum(p, axis=0, keepdims=True)
                acc = acc + _dot(vt_ref[hsl(g), lo:hi], p.astype(BF16))
            o_ref[:, hsl(g)] = (acc * (1.0 / l)).T.astype(BF16)

        pending = scores(0)
        for g in range(heads):
            upcoming = scores(g + 1) if g + 1 < heads else None
            weigh(g, *pending)
            pending = upcoming

    step = MOBA_PAST_STEP
    for n_past in range(0, n_blocks + step, step):
        n_past = min(n_past, n_blocks)
        lo = n_past - step + 1 if n_past else 0
        pl.when(jnp.logical_and(i >= lo, i <= n_past))(lambda n=n_past: attend(n))
        if n_past == n_blocks:
            break


def _moba(proj, vt, batch, seq):
    t = proj.shape[0]
    nb = seq // MOBA_BLOCK
    g = MOBA_HEADS
    gw = g * HEAD_DIM
    k0 = N_HEADS // g
    return pl.pallas_call(
        _moba_kernel,
        grid=(batch, N_HEADS // g, nb),
        in_specs=[
            pl.BlockSpec((MOBA_BLOCK, gw), lambda b, h, i: (b * nb + i, h)),
            pl.BlockSpec((MOBA_BLOCK, gw), lambda b, h, i: (b * nb + i, k0 + h)),
            pl.BlockSpec((gw, MOBA_BLOCK), lambda b, h, i: (h, b * nb + i)),
            pl.BlockSpec((seq, gw), lambda b, h, i: (b, k0 + h)),
            pl.BlockSpec((gw, seq), lambda b, h, i: (h, b)),
        ],
        out_specs=pl.BlockSpec((MOBA_BLOCK, gw), lambda b, h, i: (b * nb + i, h)),
        out_shape=jax.ShapeDtypeStruct((t, N_HEADS * HEAD_DIM), BF16),
        scratch_shapes=[pltpu.VMEM((g, nb, HEAD_DIM), BF16),
                        pltpu.VMEM((g, nb, HEAD_DIM), BF16),
                        pltpu.VMEM((g, seq, MOBA_BLOCK), F32)],
        compiler_params=pltpu.CompilerParams(
            dimension_semantics=("arbitrary", "arbitrary", "arbitrary"),
            vmem_limit_bytes=VMEM_LIMIT),
        name="moba",
    )(proj, proj, vt, proj, vt)


def _chunk_masks(n):
    row = lax.broadcasted_iota(jnp.int32, (n, n), 0)
    col = lax.broadcasted_iota(jnp.int32, (n, n), 1)
    shift = GDN_CHUNK.bit_length() - 1
    same = (row >> shift) == (col >> shift)
    return row, col, same


def _split3(x):
    hi = x.astype(BF16)
    rest = x - hi.astype(F32)
    mid = rest.astype(BF16)
    lo = (rest - mid.astype(F32)).astype(BF16)
    return hi, mid, lo


def _gdn_gates(ab, abt, alog_col, dt_col):
    r = ab.shape[0]
    nh = N_HEADS
    row, col, same = _chunk_masks(r)
    triu = jnp.where(jnp.logical_and(same, row <= col), 1.0, 0.0).astype(BF16)
    ones = jnp.where(same, 1.0, 0.0).astype(BF16)

    srow = lax.broadcasted_iota(jnp.int32, abt.shape, 0)
    gt = -jnp.exp(alog_col) * _softplus(abt + dt_col)
    gt = jnp.where(srow >= nh, gt, 0.0)
    pieces = _split3(gt)
    gc_t = sum(_dot(p, triu) for p in pieces)
    gl_t = sum(_dot(p, ones) for p in pieces)

    stack = jnp.concatenate(
        [gc_t[nh:2 * nh, :], gl_t[nh:2 * nh, :], jnp.zeros((HEAD_DIM - 2 * nh, r), F32)], axis=0)
    shifted = pltpu.roll(stack.T, nh, axis=1)
    lane = lax.broadcasted_iota(jnp.int32, ab.shape, 1)
    return jnp.where(lane < nh, _sigmoid(ab), shifted), gc_t


def _gdn_kernel(q_ref, k_ref, v_ref, z_ref, colf_ref, rowf_ref, cw_ref, gn_ref,
                o_ref, xe_ref, state_ref, cp_ref, at_ref, x_ref, qe_ref, kd_ref, colp_ref):
    sblk = pl.program_id(1)
    r = q_ref.shape[0]
    heads = q_ref.shape[1] // HEAD_DIM
    n_chunks = r // GDN_CHUNK
    n_levels = (GDN_CHUNK - 1).bit_length()
    pad = 8
    hsl = lambda g: slice(g * HEAD_DIM, (g + 1) * HEAD_DIM)
    handover = (cp_ref, at_ref, x_ref, qe_ref, kd_ref, colp_ref)

    @pl.when(jnp.logical_and(pl.program_id(0) == 0, sblk == 0))
    def _():
        for ref in handover:
            ref[...] = jnp.zeros_like(ref)
        state_ref[...] = jnp.zeros_like(state_ref)

    @pl.when(sblk == 0)
    def _():
        xe_ref[...] = jnp.zeros_like(xe_ref)

    def prepare(slot):
        def conv_silu(c, raw_ref, g):
            x = raw_ref[:, hsl(g)].astype(F32)
            xe = jnp.concatenate([xe_ref[c, :, hsl(g)], x], axis=0)
            w = cw_ref[c][:, hsl(g)]
            y = x * w[CONV_WIDTH - 1:CONV_WIDTH, :]
            for d in range(1, CONV_WIDTH):
                y = y + pltpu.roll(xe, d, axis=0)[pad:, :] * w[CONV_WIDTH - 1 - d:CONV_WIDTH - d, :]
            xe_ref[c, :, hsl(g)] = x[r - pad:r, :]
            return _silu(y)

        row, col, same = _chunk_masks(r)
        causal = jnp.logical_and(same, col <= row)
        strict = jnp.logical_and(same, col < row)
        colf = colf_ref[...]
        colp_ref[slot] = colf
        for g in range(heads):
            q = conv_silu(0, q_ref, g)
            k = conv_silu(1, k_ref, g)
            v = conv_silu(2, v_ref, g)
            q = q * lax.rsqrt(jnp.sum(q * q, axis=-1, keepdims=True) + EPS) * (HEAD_DIM ** -0.5)
            k = k * lax.rsqrt(jnp.sum(k * k, axis=-1, keepdims=True) + EPS)

            beta = colf[:, g:g + 1]
            gc = colf[:, heads + g:heads + g + 1]
            glast = colf[:, 2 * heads + g:2 * heads + g + 1]
            gc_row = rowf_ref[heads + g:heads + g + 1, :]
            decay = jnp.exp(jnp.where(causal, gc - gc_row, -jnp.inf))

            kb = k * beta
            k16 = k.astype(BF16)
            cp_ref[slot, g] = jnp.where(
                strict, -_dot_nt(kb.astype(BF16), k16) * decay, 0.0).astype(BF16)
            at_ref[slot, g] = (_dot_nt(q.astype(BF16), k16) * decay).astype(BF16)
            egc = jnp.exp(gc)
            x_ref[slot, g] = jnp.concatenate([v * beta, kb * egc], axis=1)
            qe_ref[slot, g] = q * egc
            kd_ref[slot, g] = (k * jnp.exp(glast - gc)).astype(BF16)

    def finish(slot):
        hd_range = range(heads)
        c_pow = [cp_ref[slot, g] for g in hd_range]
        x = [x_ref[slot, g] for g in hd_range]
        for level in range(n_levels):
            for g in hd_range:
                x[g] = x[g] + _dot(c_pow[g], x[g].astype(BF16))
                if level + 1 < n_levels:
                    c_pow[g] = _dot(c_pow[g], c_pow[g]).astype(BF16)

        p_mat, r_mat, nu_g = [], [], []
        for g in hd_range:
            sol16 = x[g].astype(BF16)
            aw_au = _dot(at_ref[slot, g], sol16)
            p_mat.append((qe_ref[slot, g] - aw_au[:, HEAD_DIM:]).astype(BF16))
            r_mat.append(aw_au[:, :HEAD_DIM])
            kd = kd_ref[slot, g]
            nu_g.append([_dot_tn(kd[c * GDN_CHUNK:(c + 1) * GDN_CHUNK, :],
                                 sol16[c * GDN_CHUNK:(c + 1) * GDN_CHUNK, :])
                         for c in range(n_chunks)])

        colp = colp_ref[slot]
        states = [state_ref[g] for g in hd_range]
        outs = [[] for _ in hd_range]
        for c in range(n_chunks):
            lo, hi = c * GDN_CHUNK, (c + 1) * GDN_CHUNK
            for g in hd_range:
                s16 = states[g].astype(BF16)
                e_c = jnp.exp(colp[lo:lo + 1, 2 * heads + g:2 * heads + g + 1])
                outs[g].append(_dot(p_mat[g][lo:hi, :], s16) + r_mat[g][lo:hi, :])
                states[g] = (e_c * states[g] - _dot(nu_g[g][c][:, HEAD_DIM:].astype(BF16), s16)
                             + nu_g[g][c][:, :HEAD_DIM])

        gnorm = gn_ref[...]
        for g in hd_range:
            state_ref[g] = jnp.where(sblk > 0, states[g], 0.0)
            o = jnp.concatenate(outs[g], axis=0)
            z = z_ref[:, hsl(g)].astype(F32)
            o_ref[:, hsl(g)] = (_rms(o, gnorm) * _silu(z)).astype(BF16)

    for parity in range(2):
        @pl.when(sblk % 2 == parity)
        def _():
            finish(1 - parity)
            prepare(parity)


def _gdn(proj, colf, rowf, conv_w3, gdn_norm, batch, seq):
    t = proj.shape[0]
    r = GDN_ROWS
    ns = seq // r
    assert ns % 2 == 0
    g = N_HEADS
    gw = g * HEAD_DIM
    qb0, kb0, vb0, z0 = 2, 3, 4, 5
    prep_blk = lambda b, s: b * ns + jnp.minimum(s, ns - 1)
    fin_blk = lambda b, s: b * ns + jnp.maximum(s - 1, 0)

    def tok(off):
        return pl.BlockSpec((r, gw), lambda b, s: (prep_blk(b, s), off))

    return pl.pallas_call(
        _gdn_kernel,
        grid=(batch, ns + 1),
        in_specs=[
            tok(qb0), tok(kb0), tok(vb0),
            pl.BlockSpec((r, gw), lambda b, s: (fin_blk(b, s), z0)),
            pl.BlockSpec((r, HEAD_DIM), lambda b, s: (prep_blk(b, s), 0)),
            pl.BlockSpec((2 * g, r), lambda b, s: (0, prep_blk(b, s))),
            pl.BlockSpec((3, CONV_WIDTH, gw), lambda b, s: (0, 0, 0)),
            pl.BlockSpec((1, HEAD_DIM), lambda b, s: (0, 0)),
        ],
        out_specs=pl.BlockSpec((r, gw), lambda b, s: (fin_blk(b, s), 0)),
        out_shape=jax.ShapeDtypeStruct((t, gw), BF16),
        scratch_shapes=[pltpu.VMEM((3, 8, gw), F32),
                        pltpu.VMEM((g, HEAD_DIM, HEAD_DIM), F32),
                        pltpu.VMEM((2, g, r, r), BF16),
                        pltpu.VMEM((2, g, r, r), BF16),
                        pltpu.VMEM((2, g, r, 2 * HEAD_DIM), F32),
                        pltpu.VMEM((2, g, r, HEAD_DIM), F32),
                        pltpu.VMEM((2, g, r, HEAD_DIM), BF16),
                        pltpu.VMEM((2, r, HEAD_DIM), F32)],
        compiler_params=pltpu.CompilerParams(
            dimension_semantics=("arbitrary", "arbitrary"),
            vmem_limit_bytes=VMEM_LIMIT),
        name="gdn",
    )(proj, proj, proj, proj, colf, rowf, conv_w3, gdn_norm)


def _merge_kernel(ya_ref, yb_ref, ga_ref, gb_ref, x_ref, wa_ref, wb_ref, wo_ref, ln_ref, o_ref):
    merged = (_sigmoid(ga_ref[...].astype(F32)) * _dot(ya_ref[...], wa_ref[...])
              + _sigmoid(gb_ref[...].astype(F32)) * _dot(yb_ref[...], wb_ref[...]))
    out = _dot(merged.astype(BF16), wo_ref[...])
    o_ref[...] = x_ref[...] + _rms(out, ln_ref[...])


def _merge(ya, yb, proj, x2, wa, wb, wo, ln, tm):
    t, d = x2.shape
    gate0 = 6
    const = lambda i: (0, 0)
    return pl.pallas_call(
        _merge_kernel,
        grid=(t // tm,),
        in_specs=[
            pl.BlockSpec((tm, d), lambda i: (i, 0)),
            pl.BlockSpec((tm, d), lambda i: (i, 0)),
            pl.BlockSpec((tm, d), lambda i: (i, gate0)),
            pl.BlockSpec((tm, d), lambda i: (i, gate0 + 1)),
            pl.BlockSpec((tm, d), lambda i: (i, 0)),
            pl.BlockSpec((d, d), const), pl.BlockSpec((d, d), const), pl.BlockSpec((d, d), const),
            pl.BlockSpec((1, d), const),
        ],
        out_specs=pl.BlockSpec((tm, d), lambda i: (i, 0)),
        out_shape=jax.ShapeDtypeStruct((t, d), F32),
        compiler_params=pltpu.CompilerParams(
            dimension_semantics=("arbitrary",), vmem_limit_bytes=VMEM_LIMIT),
        name="merge",
    )(ya, yb, proj, proj, x2, wa, wb, wo, ln)


def _ffn_ple_kernel(h_ref, p_ref, ln1_ref, wg_ref, wu_ref, wd_ref, ln2_ref,
                    wp_ref, lnp_ref, wpg_ref, o_ref):
    h1 = h_ref[...]
    f = _rms(h1, ln1_ref[...]).astype(BF16)
    act = (_silu(_dot(f, wg_ref[...])) * _dot(f, wu_ref[...])).astype(BF16)
    h2 = h1 + _rms(_dot(act, wd_ref[...]), ln2_ref[...])
    e = _rms(_dot(p_ref[...].astype(BF16), wp_ref[...]), lnp_ref[...])
    o_ref[...] = h2 + _sigmoid(_dot(h2.astype(BF16), wpg_ref[...])) * e


def _ffn_ple(h1, p2, ln1, wg, wu, wd, ln2, wp, lnp, wpg, tm):
    t, d = h1.shape
    dff = wg.shape[1]
    dp = p2.shape[1]
    const = lambda i: (0, 0)
    once = pl.Buffered(1)

    def resident(shape):
        return pl.BlockSpec(shape, const, pipeline_mode=once)

    return pl.pallas_call(
        _ffn_ple_kernel,
        grid=(t // tm,),
        in_specs=[
            pl.BlockSpec((tm, d), lambda i: (i, 0)),
            pl.BlockSpec((tm, dp), lambda i: (i, 0)),
            resident((1, d)), resident((d, dff)), resident((d, dff)), resident((dff, d)),
            resident((1, d)), resident((dp, d)), resident((1, d)), resident((d, d)),
        ],
        out_specs=pl.BlockSpec((tm, d), lambda i: (i, 0)),
        out_shape=jax.ShapeDtypeStruct((t, d), F32),
        compiler_params=pltpu.CompilerParams(
            dimension_semantics=("arbitrary",), vmem_limit_bytes=VMEM_LIMIT),
        name="ffn_ple",
    )(h1, p2, ln1, wg, wu, wd, ln2, wp, lnp, wpg)


def _rope_tables(seq):
    inv = 1.0 / (ROPE_THETA ** (jnp.arange(0, HEAD_DIM, 2, dtype=F32) / HEAD_DIM))
    ang = jnp.arange(seq, dtype=F32)[:, None] * inv[None, :]
    cos, sin = jnp.cos(ang), jnp.sin(ang)
    return jnp.concatenate([cos, cos], axis=1), jnp.concatenate([-sin, sin], axis=1)


def _layer(h, p_i, ln_pre_mix, w_in, conv_w, a_log, dt_bias, gdn_norm, w_proj_a, w_proj_b,
           w_out, ln_post_mix, ln_pre_ffn, w_ffn_gate, w_ffn_up, w_ffn_down, ln_post_ffn,
           w_ple, ln_ple, w_ple_gate):
    b, s, d = h.shape
    t = b * s
    width = N_HEADS * HEAD_DIM
    assert d == width and s % MOBA_BLOCK == 0 and s % GDN_ROWS == 0
    x2 = h.reshape(t, d)
    row = lambda v: v.reshape(1, -1).astype(F32)

    c_small = 7 * width
    c_gate = c_small + 2 * N_HEADS
    w_main = jnp.concatenate(
        [w_in[:, :2 * width], w_in[:, 3 * width:c_small], w_in[:, c_gate:]], axis=1).astype(BF16)
    w_vt = w_in[:, 2 * width:3 * width].T.astype(BF16)
    w_small = jnp.pad(w_in[:, c_small:c_gate], ((0, 0), (0, HEAD_DIM - 2 * N_HEADS))).astype(BF16)
    cosf, sinf = _rope_tables(s)
    tm_in = min(1024, s)
    decay_rows = lambda v: jnp.pad(v.astype(F32), (N_HEADS, 0)).reshape(-1, 1)
    proj, vt, colf, rowf = _inproj(
        x2, row(ln_pre_mix), w_main, w_vt, w_small, w_small[:, :2 * N_HEADS].T, cosf, sinf,
        decay_rows(a_log), decay_rows(dt_bias), s, tm_in)

    y_a = _moba(proj, vt, b, s)

    conv_w3 = conv_w.astype(F32).reshape(CONV_WIDTH, 3, width).transpose(1, 0, 2)
    y_b = _gdn(proj, colf, rowf, conv_w3, row(gdn_norm), b, s)

    h1 = _merge(y_a, y_b, proj, x2, w_proj_a.astype(BF16), w_proj_b.astype(BF16),
                w_out.astype(BF16), row(ln_post_mix), min(1024, t))
    out = _ffn_ple(h1, p_i.reshape(t, -1), row(ln_pre_ffn), w_ffn_gate.astype(BF16),
                   w_ffn_up.astype(BF16), w_ffn_down.astype(BF16), row(ln_post_ffn),
                   w_ple.astype(BF16), row(ln_ple), w_ple_gate.astype(BF16), min(512, t))
    return out.reshape(b, s, d)


def kernel(x, p, ln_pre_mix, w_in, conv_w, a_log, dt_bias, gdn_norm, w_proj_a, w_proj_b, w_out,
           ln_post_mix, ln_pre_ffn, w_ffn_gate, w_ffn_up, w_ffn_down, ln_post_ffn, w_ple, ln_ple,
           w_ple_gate):
    h = x
    for i in range(p.shape[0]):
        h = _layer(h, p[i], ln_pre_mix[i], w_in[i], conv_w[i], a_log[i], dt_bias[i], gdn_norm[i],
                   w_proj_a[i], w_proj_b[i], w_out[i], ln_post_mix[i], ln_pre_ffn[i],
                   w_ffn_gate[i], w_ffn_up[i], w_ffn_down[i], ln_post_ffn[i], w_ple[i],
                   ln_ple[i], w_ple_gate[i])
    return h
```

```python
import itertools

import jax
import jax.numpy as jnp
from jax import lax
from jax.experimental import pallas as pl
from jax.experimental.pallas import tpu as pltpu

F32 = jnp.float32
BF16 = jnp.bfloat16

HEAD_DIM = 128
N_HEADS = 8
INPROJ_CHUNK = 256
MOBA_BLOCK = 256
MOBA_TOPK = 3
MOBA_HEADS = 4
MOBA_PAST_STEP = 2
GDN_CHUNK = 64
GDN_ROWS = 256
CONV_WIDTH = 4
EPS = 1e-6
ROPE_THETA = 10000.0
NEG_BIG = -1e30
LOG2_E = 1.4426950408889634
VMEM_LIMIT = 56 * 1024 * 1024

_NT = (((1,), (1,)), ((), ()))
_TN = (((0,), (0,)), ((), ()))


def _dot(a, b):
    return jnp.dot(a, b, preferred_element_type=F32)


def _dot_nt(a, b):
    return lax.dot_general(a, b, _NT, preferred_element_type=F32)


def _dot_tn(a, b):
    return lax.dot_general(a, b, _TN, preferred_element_type=F32)


def _rms(x, w):
    return x * lax.rsqrt(jnp.mean(x * x, axis=-1, keepdims=True) + EPS) * w


def _sigmoid(x):
    return 1.0 / (1.0 + jnp.exp(-x))


def _silu(x):
    return x * _sigmoid(x)


def _softplus(x):
    return jnp.maximum(x, 0.0) + jnp.log1p(jnp.exp(-jnp.abs(x)))


def _inproj_kernel(x_ref, ln_ref, w_ref, wvt_ref, ws_ref, wst_ref, cos_ref, sin_ref,
                   alog_ref, dt_ref, o_ref, vt_ref, colf_ref, rowf_ref, u_ref):
    j = pl.program_id(1)

    @pl.when(j == 0)
    def _():
        u = _rms(x_ref[...], ln_ref[...]).astype(BF16)
        u_ref[...] = u
        ab = _dot(u, ws_ref[...])
        abt = _dot_nt(wst_ref[...], u)
        for r0 in range(0, u.shape[0], GDN_ROWS):
            colf, rowf = _gdn_gates(ab[r0:r0 + GDN_ROWS, :], abt[:, r0:r0 + GDN_ROWS],
                                    alog_ref[...], dt_ref[...])
            colf_ref[r0:r0 + GDN_ROWS, :] = colf
            rowf_ref[:, r0:r0 + GDN_ROWS] = rowf
        for c in range(vt_ref.shape[0] // INPROJ_CHUNK):
            rows = slice(c * INPROJ_CHUNK, (c + 1) * INPROJ_CHUNK)
            vt_ref[rows, :] = _dot_nt(wvt_ref[rows, :], u).astype(BF16)

    n_chunks = o_ref.shape[1] // INPROJ_CHUNK
    chunk = lambda c: slice(c * INPROJ_CHUNK, (c + 1) * INPROJ_CHUNK)

    @pl.when(j == 0)
    def _():
        u = u_ref[...]
        cos_k, sin_k = cos_ref[...], sin_ref[...]
        qscale = LOG2_E * HEAD_DIM ** -0.5
        cos_q, sin_q = cos_k * qscale, sin_k * qscale
        for c in range(n_chunks):
            cosf, sinf = (cos_q, sin_q) if c < n_chunks // 2 else (cos_k, sin_k)
            acc = _dot(u, w_ref[:, chunk(c)])
            for h in range(INPROJ_CHUNK // HEAD_DIM):
                seg = acc[:, h * HEAD_DIM:(h + 1) * HEAD_DIM]
                rot = seg * cosf + pltpu.roll(seg, HEAD_DIM // 2, axis=1) * sinf
                lo = c * INPROJ_CHUNK + h * HEAD_DIM
                o_ref[:, lo:lo + HEAD_DIM] = rot.astype(BF16)

    @pl.when(j >= 1)
    def _():
        u = u_ref[...]
        for c in range(n_chunks):
            o_ref[:, chunk(c)] = _dot(u, w_ref[:, chunk(c)]).astype(BF16)


def _inproj(x2, ln, w_main, w_vt, w_small, w_small_t, cosf, sinf, alog_col, dt_col, seq, tm):
    t, d = x2.shape
    n_main = w_main.shape[1]
    n_small = w_small.shape[1]
    n_rows = w_small_t.shape[0]
    width = N_HEADS * HEAD_DIM
    tn = 2 * width
    n_seq_tiles = seq // tm
    assert tm % GDN_ROWS == 0 and n_main % tn == 0
    return pl.pallas_call(
        _inproj_kernel,
        grid=(t // tm, n_main // tn),
        in_specs=[
            pl.BlockSpec((tm, d), lambda i, j: (i, 0)),
            pl.BlockSpec((1, d), lambda i, j: (0, 0)),
            pl.BlockSpec((d, tn), lambda i, j: (0, j)),
            pl.BlockSpec((width, d), lambda i, j: (0, 0)),
            pl.BlockSpec((d, n_small), lambda i, j: (0, 0)),
            pl.BlockSpec((n_rows, d), lambda i, j: (0, 0)),
            pl.BlockSpec((tm, HEAD_DIM), lambda i, j: (i % n_seq_tiles, 0)),
            pl.BlockSpec((tm, HEAD_DIM), lambda i, j: (i % n_seq_tiles, 0)),
            pl.BlockSpec((n_rows, 1), lambda i, j: (0, 0)),
            pl.BlockSpec((n_rows, 1), lambda i, j: (0, 0)),
        ],
        out_specs=[
            pl.BlockSpec((tm, tn), lambda i, j: (i, j)),
            pl.BlockSpec((width, tm), lambda i, j: (0, i)),
            pl.BlockSpec((tm, n_small), lambda i, j: (i, 0)),
            pl.BlockSpec((n_rows, tm), lambda i, j: (0, i)),
        ],
        out_shape=[
            jax.ShapeDtypeStruct((t, n_main), BF16),
            jax.ShapeDtypeStruct((width, t), BF16),
            jax.ShapeDtypeStruct((t, n_small), F32),
            jax.ShapeDtypeStruct((n_rows, t), F32),
        ],
        scratch_shapes=[pltpu.VMEM((tm, d), BF16)],
        compiler_params=pltpu.CompilerParams(
            dimension_semantics=("arbitrary", "arbitrary"), vmem_limit_bytes=VMEM_LIMIT),
        name="inproj",
    )(x2, ln, w_main, w_vt, w_small, w_small_t, cosf, sinf, alog_col, dt_col)


def _moba_kernel(q_ref, kd_ref, vtd_ref, k_ref, vt_ref, o_ref, kmh_ref, kml_ref, s_ref):
    i = pl.program_id(2)
    blk = MOBA_BLOCK
    n_blocks = k_ref.shape[0] // blk
    heads = q_ref.shape[1] // HEAD_DIM
    hsl = lambda g: slice(g * HEAD_DIM, (g + 1) * HEAD_DIM)

    @pl.when(i == 0)
    def _():
        for g in range(heads):
            km = jnp.mean(k_ref[:, hsl(g)].astype(F32).reshape(n_blocks, blk, HEAD_DIM), axis=1)
            hi = km.astype(BF16)
            kmh_ref[g] = hi
            kml_ref[g] = (km - hi.astype(F32)).astype(BF16)

    def attend(n_past):
        sub = lax.broadcasted_iota(jnp.int32, (n_blocks, blk), 0)
        sub_f = sub.astype(F32)
        past = sub < i
        krow = lax.broadcasted_iota(jnp.int32, (blk, blk), 0)
        qcol = lax.broadcasted_iota(jnp.int32, (blk, blk), 1)
        causal = krow <= qcol
        nk = n_past * blk
        halves = ((0, nk // 2), (nk // 2, nk)) if n_past else ()

        def scores(g):
            q = q_ref[:, hsl(g)]
            s_d = jnp.where(causal, _dot_nt(kd_ref[:, hsl(g)], q), -jnp.inf)
            m = jnp.max(s_d, axis=0, keepdims=True)
            if n_past:
                gate = _dot_nt(kmh_ref[g], q) + _dot_nt(kml_ref[g], q)
                gsel = jnp.where(past, gate, -jnp.inf)
                sel = jnp.zeros((n_blocks, blk), jnp.bool_)
                for _ in range(MOBA_TOPK):
                    mx = jnp.max(gsel, axis=0, keepdims=True)
                    idx = jnp.min(jnp.where(gsel == mx, sub_f, float(n_blocks)),
                                  axis=0, keepdims=True)
                    hit = sub_f == idx
                    sel = jnp.logical_or(sel, hit)
                    gsel = jnp.where(hit, -jnp.inf, gsel)
                bias = jnp.where(jnp.logical_and(sel, past), 0.0, NEG_BIG)
                for lo, hi in halves:
                    s_ref[g,lo:hi, :] = _dot_nt(k_ref[lo:hi, hsl(g)], q)
                for c in range(n_past):
                    m_c = jnp.max(s_ref[g,c * blk:(c + 1) * blk, :], axis=0, keepdims=True)
                    m = jnp.maximum(m, m_c + bias[c:c + 1, :])
            else:
                bias = None
            return s_d, m, bias

        def weigh(g, s_d, m, bias):
            p_d = jnp.exp2(s_d - m)
            l = jnp.sum(p_d, axis=0, keepdims=True)
            acc = _dot(vtd_ref[hsl(g), :], p_d.astype(BF16))
            for lo, hi in halves:
                p = jnp.concatenate(
                    [jnp.exp2(s_ref[g,c * blk:(c + 1) * blk, :] + (bias[c:c + 1, :] - m))
                     for c in range(lo // blk, hi // blk)], axis=0)
                l = l + jnp.sum(p, axis=0, keepdims=True)
                acc = acc + _dot(vt_ref[hsl(g), lo:hi], p.astype(BF16))
            o_ref[:, hsl(g)] = (acc * (1.0 / l)).T.astype(BF16)

        pending = scores(0)
        for g in range(heads):
            upcoming = scores(g + 1) if g + 1 < heads else None
            weigh(g, *pending)
            pending = upcoming

    step = MOBA_PAST_STEP
    for n_past in range(0, n_blocks + step, step):
        n_past = min(n_past, n_blocks)
        lo = n_past - step + 1 if n_past else 0
        pl.when(jnp.logical_and(i >= lo, i <= n_past))(lambda n=n_past: attend(n))
        if n_past == n_blocks:
            break


def _moba(proj, vt, batch, seq):
    t = proj.shape[0]
    nb = seq // MOBA_BLOCK
    g = MOBA_HEADS
    gw = g * HEAD_DIM
    k0 = N_HEADS // g
    return pl.pallas_call(
        _moba_kernel,
        grid=(batch, N_HEADS // g, nb),
        in_specs=[
            pl.BlockSpec((MOBA_BLOCK, gw), lambda b, h, i: (b * nb + i, h)),
            pl.BlockSpec((MOBA_BLOCK, gw), lambda b, h, i: (b * nb + i, k0 + h)),
            pl.BlockSpec((gw, MOBA_BLOCK), lambda b, h, i: (h, b * nb + i)),
            pl.BlockSpec((seq, gw), lambda b, h, i: (b, k0 + h)),
            pl.BlockSpec((gw, seq), lambda b, h, i: (h, b)),
        ],
        out_specs=pl.BlockSpec((MOBA_BLOCK, gw), lambda b, h, i: (b * nb + i, h)),
        out_shape=jax.ShapeDtypeStruct((t, N_HEADS * HEAD_DIM), BF16),
        scratch_shapes=[pltpu.VMEM((g, nb, HEAD_DIM), BF16),
                        pltpu.VMEM((g, nb, HEAD_DIM), BF16),
                        pltpu.VMEM((g, seq, MOBA_BLOCK), F32)],
        compiler_params=pltpu.CompilerParams(
            dimension_semantics=("arbitrary", "arbitrary", "arbitrary"),
            vmem_limit_bytes=VMEM_LIMIT),
        name="moba",
    )(proj, proj, vt, proj, vt)


def _chunk_masks(n):
    row = lax.broadcasted_iota(jnp.int32, (n, n), 0)
    col = lax.broadcasted_iota(jnp.int32, (n, n), 1)
    shift = GDN_CHUNK.bit_length() - 1
    same = (row >> shift) == (col >> shift)
    return row, col, same


def _split3(x):
    hi = x.astype(BF16)
    rest = x - hi.astype(F32)
    mid = rest.astype(BF16)
    lo = (rest - mid.astype(F32)).astype(BF16)
    return hi, mid, lo


def _gdn_gates(ab, abt, alog_col, dt_col):
    r = ab.shape[0]
    nh = N_HEADS
    row, col, same = _chunk_masks(r)
    triu = jnp.where(jnp.logical_and(same, row <= col), 1.0, 0.0).astype(BF16)
    ones = jnp.where(same, 1.0, 0.0).astype(BF16)

    srow = lax.broadcasted_iota(jnp.int32, abt.shape, 0)
    gt = -jnp.exp(alog_col) * _softplus(abt + dt_col)
    gt = jnp.where(srow >= nh, gt, 0.0)
    pieces = _split3(gt)
    gc_t = sum(_dot(p, triu) for p in pieces)
    gl_t = sum(_dot(p, ones) for p in pieces)

    stack = jnp.concatenate(
        [gc_t[nh:2 * nh, :], gl_t[nh:2 * nh, :], jnp.zeros((HEAD_DIM - 2 * nh, r), F32)], axis=0)
    shifted = pltpu.roll(stack.T, nh, axis=1)
    lane = lax.broadcasted_iota(jnp.int32, ab.shape, 1)
    return jnp.where(lane < nh, _sigmoid(ab), shifted), gc_t


def _gdn_kernel(q_ref, k_ref, v_ref, z_ref, colf_ref, rowf_ref, cw_ref, gn_ref,
                o_ref, xe_ref, state_ref, cp_ref, at_ref, x_ref, qe_ref, kd_ref, colp_ref):
    sblk = pl.program_id(1)
    r = q_ref.shape[0]
    heads = q_ref.shape[1] // HEAD_DIM
    n_chunks = r // GDN_CHUNK
    n_levels = (GDN_CHUNK - 1).bit_length()
    pad = 8
    hsl = lambda g: slice(g * HEAD_DIM, (g + 1) * HEAD_DIM)
    handover = (cp_ref, at_ref, x_ref, qe_ref, kd_ref, colp_ref)

    @pl.when(jnp.logical_and(pl.program_id(0) == 0, sblk == 0))
    def _():
        for ref in handover:
            ref[...] = jnp.zeros_like(ref)
        state_ref[...] = jnp.zeros_like(state_ref)

    @pl.when(sblk == 0)
    def _():
        xe_ref[...] = jnp.zeros_like(xe_ref)

    def prepare(slot):
        def conv_silu(c, raw_ref, g):
            x = raw_ref[:, hsl(g)].astype(F32)
            xe = jnp.concatenate([xe_ref[c, :, hsl(g)], x], axis=0)
            w = cw_ref[c][:, hsl(g)]
            y = x * w[CONV_WIDTH - 1:CONV_WIDTH, :]
            for d in range(1, CONV_WIDTH):
                y = y + pltpu.roll(xe, d, axis=0)[pad:, :] * w[CONV_WIDTH - 1 - d:CONV_WIDTH - d, :]
            xe_ref[c, :, hsl(g)] = x[r - pad:r, :]
            return _silu(y)

        row, col, same = _chunk_masks(r)
        causal = jnp.logical_and(same, col <= row)
        strict = jnp.logical_and(same, col < row)
        colf = colf_ref[...]
        colp_ref[slot] = colf
        for g in range(heads):
            q = conv_silu(0, q_ref, g)
            k = conv_silu(1, k_ref, g)
            v = conv_silu(2, v_ref, g)
            q = q * lax.rsqrt(jnp.sum(q * q, axis=-1, keepdims=True) + EPS) * (HEAD_DIM ** -0.5)
            k = k * lax.rsqrt(jnp.sum(k * k, axis=-1, keepdims=True) + EPS)

            beta = colf[:, g:g + 1]
            gc = colf[:, heads + g:heads + g + 1]
            glast = colf[:, 2 * heads + g:2 * heads + g + 1]
            gc_row = rowf_ref[heads + g:heads + g + 1, :]
            decay = jnp.exp(jnp.where(causal, gc - gc_row, -jnp.inf))

            kb = k * beta
            k16 = k.astype(BF16)
            cp_ref[slot, g] = jnp.where(
                strict, -_dot_nt(kb.astype(BF16), k16) * decay, 0.0).astype(BF16)
            at_ref[slot, g] = (_dot_nt(q.astype(BF16), k16) * decay).astype(BF16)
            egc = jnp.exp(gc)
            x_ref[slot, g] = jnp.concatenate([v * beta, kb * egc], axis=1)
            qe_ref[slot, g] = q * egc
            kd_ref[slot, g] = (k * jnp.exp(glast - gc)).astype(BF16)
            yield

    def finish(slot):
        hd_range = range(heads)
        c_pow = [cp_ref[slot, g] for g in hd_range]
        x = [x_ref[slot, g] for g in hd_range]
        for level in range(n_levels):
            for g in hd_range:
                x[g] = x[g] + _dot(c_pow[g], x[g].astype(BF16))
                if level + 1 < n_levels:
                    c_pow[g] = _dot(c_pow[g], c_pow[g]).astype(BF16)
            yield

        p_mat, r_mat, nu_g = [], [], []
        for g in hd_range:
            sol16 = x[g].astype(BF16)
            aw_au = _dot(at_ref[slot, g], sol16)
            p_mat.append((qe_ref[slot, g] - aw_au[:, HEAD_DIM:]).astype(BF16))
            r_mat.append(aw_au[:, :HEAD_DIM])
            kd = kd_ref[slot, g]
            nu_g.append([_dot_tn(kd[c * GDN_CHUNK:(c + 1) * GDN_CHUNK, :],
                                 sol16[c * GDN_CHUNK:(c + 1) * GDN_CHUNK, :])
                         for c in range(n_chunks)])
        yield

        colp = colp_ref[slot]
        states = [state_ref[g] for g in hd_range]
        outs = [[] for _ in hd_range]
        for c in range(n_chunks):
            lo, hi = c * GDN_CHUNK, (c + 1) * GDN_CHUNK
            for g in hd_range:
                s16 = states[g].astype(BF16)
                e_c = jnp.exp(colp[lo:lo + 1, 2 * heads + g:2 * heads + g + 1])
                outs[g].append(_dot(p_mat[g][lo:hi, :], s16) + r_mat[g][lo:hi, :])
                states[g] = (e_c * states[g] - _dot(nu_g[g][c][:, HEAD_DIM:].astype(BF16), s16)
                             + nu_g[g][c][:, :HEAD_DIM])
            yield

        gnorm = gn_ref[...]
        for g in hd_range:
            state_ref[g] = jnp.where(sblk > 0, states[g], 0.0)
            o = jnp.concatenate(outs[g], axis=0)
            z = z_ref[:, hsl(g)].astype(F32)
            o_ref[:, hsl(g)] = (_rms(o, gnorm) * _silu(z)).astype(BF16)

    slot_now = sblk % 2
    for _ in itertools.zip_longest(finish(1 - slot_now), prepare(slot_now)):
        pass


def _gdn(proj, colf, rowf, conv_w3, gdn_norm, batch, seq):
    t = proj.shape[0]
    r = GDN_ROWS
    ns = seq // r
    assert ns % 2 == 0
    g = N_HEADS
    gw = g * HEAD_DIM
    qb0, kb0, vb0, z0 = 2, 3, 4, 5
    prep_blk = lambda b, s: b * ns + jnp.minimum(s, ns - 1)
    fin_blk = lambda b, s: b * ns + jnp.maximum(s - 1, 0)

    def tok(off):
        return pl.BlockSpec((r, gw), lambda b, s: (prep_blk(b, s), off))

    return pl.pallas_call(
        _gdn_kernel,
        grid=(batch, ns + 1),
        in_specs=[
            tok(qb0), tok(kb0), tok(vb0),
            pl.BlockSpec((r, gw), lambda b, s: (fin_blk(b, s), z0)),
            pl.BlockSpec((r, HEAD_DIM), lambda b, s: (prep_blk(b, s), 0)),
            pl.BlockSpec((2 * g, r), lambda b, s: (0, prep_blk(b, s))),
            pl.BlockSpec((3, CONV_WIDTH, gw), lambda b, s: (0, 0, 0)),
            pl.BlockSpec((1, HEAD_DIM), lambda b, s: (0, 0)),
        ],
        out_specs=pl.BlockSpec((r, gw), lambda b, s: (fin_blk(b, s), 0)),
        out_shape=jax.ShapeDtypeStruct((t, gw), BF16),
        scratch_shapes=[pltpu.VMEM((3, 8, gw), F32),
                        pltpu.VMEM((g, HEAD_DIM, HEAD_DIM), F32),
                        pltpu.VMEM((2, g, r, r), BF16),
                        pltpu.VMEM((2, g, r, r), BF16),
                        pltpu.VMEM((2, g, r, 2 * HEAD_DIM), F32),
                        pltpu.VMEM((2, g, r, HEAD_DIM), F32),
                        pltpu.VMEM((2, g, r, HEAD_DIM), BF16),
                        pltpu.VMEM((2, r, HEAD_DIM), F32)],
        compiler_params=pltpu.CompilerParams(
            dimension_semantics=("arbitrary", "arbitrary"),
            vmem_limit_bytes=VMEM_LIMIT),
        name="gdn",
    )(proj, proj, proj, proj, colf, rowf, conv_w3, gdn_norm)


def _merge_kernel(ya_ref, yb_ref, ga_ref, gb_ref, x_ref, wa_ref, wb_ref, wo_ref, ln_ref, o_ref):
    merged = (_sigmoid(ga_ref[...].astype(F32)) * _dot(ya_ref[...], wa_ref[...])
              + _sigmoid(gb_ref[...].astype(F32)) * _dot(yb_ref[...], wb_ref[...]))
    out = _dot(merged.astype(BF16), wo_ref[...])
    o_ref[...] = x_ref[...] + _rms(out, ln_ref[...])


def _merge(ya, yb, proj, x2, wa, wb, wo, ln, tm):
    t, d = x2.shape
    gate0 = 6
    const = lambda i: (0, 0)
    return pl.pallas_call(
        _merge_kernel,
        grid=(t // tm,),
        in_specs=[
            pl.BlockSpec((tm, d), lambda i: (i, 0)),
            pl.BlockSpec((tm, d), lambda i: (i, 0)),
            pl.BlockSpec((tm, d), lambda i: (i, gate0)),
            pl.BlockSpec((tm, d), lambda i: (i, gate0 + 1)),
            pl.BlockSpec((tm, d), lambda i: (i, 0)),
            pl.BlockSpec((d, d), const), pl.BlockSpec((d, d), const), pl.BlockSpec((d, d), const),
            pl.BlockSpec((1, d), const),
        ],
        out_specs=pl.BlockSpec((tm, d), lambda i: (i, 0)),
        out_shape=jax.ShapeDtypeStruct((t, d), F32),
        compiler_params=pltpu.CompilerParams(
            dimension_semantics=("arbitrary",), vmem_limit_bytes=VMEM_LIMIT),
        name="merge",
    )(ya, yb, proj, proj, x2, wa, wb, wo, ln)


def _ffn_ple_kernel(h_ref, p_ref, ln1_ref, wg_ref, wu_ref, wd_ref, ln2_ref,
                    wp_ref, lnp_ref, wpg_ref, o_ref):
    h1 = h_ref[...]
    f = _rms(h1, ln1_ref[...]).astype(BF16)
    act = (_silu(_dot(f, wg_ref[...])) * _dot(f, wu_ref[...])).astype(BF16)
    h2 = h1 + _rms(_dot(act, wd_ref[...]), ln2_ref[...])
    e = _rms(_dot(p_ref[...].astype(BF16), wp_ref[...]), lnp_ref[...])
    o_ref[...] = h2 + _sigmoid(_dot(h2.astype(BF16), wpg_ref[...])) * e


def _ffn_ple(h1, p2, ln1, wg, wu, wd, ln2, wp, lnp, wpg, tm):
    t, d = h1.shape
    dff = wg.shape[1]
    dp = p2.shape[1]
    const = lambda i: (0, 0)
    once = pl.Buffered(1)

    def resident(shape):
        return pl.BlockSpec(shape, const, pipeline_mode=once)

    return pl.pallas_call(
        _ffn_ple_kernel,
        grid=(t // tm,),
        in_specs=[
            pl.BlockSpec((tm, d), lambda i: (i, 0)),
            pl.BlockSpec((tm, dp), lambda i: (i, 0)),
            resident((1, d)), resident((d, dff)), resident((d, dff)), resident((dff, d)),
            resident((1, d)), resident((dp, d)), resident((1, d)), resident((d, d)),
        ],
        out_specs=pl.BlockSpec((tm, d), lambda i: (i, 0)),
        out_shape=jax.ShapeDtypeStruct((t, d), F32),
        compiler_params=pltpu.CompilerParams(
            dimension_semantics=("arbitrary",), vmem_limit_bytes=VMEM_LIMIT),
        name="ffn_ple",
    )(h1, p2, ln1, wg, wu, wd, ln2, wp, lnp, wpg)


def _rope_tables(seq):
    inv = 1.0 / (ROPE_THETA ** (jnp.arange(0, HEAD_DIM, 2, dtype=F32) / HEAD_DIM))
    ang = jnp.arange(seq, dtype=F32)[:, None] * inv[None, :]
    cos, sin = jnp.cos(ang), jnp.sin(ang)
    return jnp.concatenate([cos, cos], axis=1), jnp.concatenate([-sin, sin], axis=1)


def _layer(h, p_i, ln_pre_mix, w_in, conv_w, a_log, dt_bias, gdn_norm, w_proj_a, w_proj_b,
           w_out, ln_post_mix, ln_pre_ffn, w_ffn_gate, w_ffn_up, w_ffn_down, ln_post_ffn,
           w_ple, ln_ple, w_ple_gate):
    b, s, d = h.shape
    t = b * s
    width = N_HEADS * HEAD_DIM
    assert d == width and s % MOBA_BLOCK == 0 and s % GDN_ROWS == 0
    x2 = h.reshape(t, d)
    row = lambda v: v.reshape(1, -1).astype(F32)

    c_small = 7 * width
    c_gate = c_small + 2 * N_HEADS
    w_main = jnp.concatenate(
        [w_in[:, :2 * width], w_in[:, 3 * width:c_small], w_in[:, c_gate:]], axis=1).astype(BF16)
    w_vt = w_in[:, 2 * width:3 * width].T.astype(BF16)
    w_small = jnp.pad(w_in[:, c_small:c_gate], ((0, 0), (0, HEAD_DIM - 2 * N_HEADS))).astype(BF16)
    cosf, sinf = _rope_tables(s)
    tm_in = min(1024, s)
    decay_rows = lambda v: jnp.pad(v.astype(F32), (N_HEADS, 0)).reshape(-1, 1)
    proj, vt, colf, rowf = _inproj(
        x2, row(ln_pre_mix), w_main, w_vt, w_small, w_small[:, :2 * N_HEADS].T, cosf, sinf,
        decay_rows(a_log), decay_rows(dt_bias), s, tm_in)

    y_a = _moba(proj, vt, b, s)

    conv_w3 = conv_w.astype(F32).reshape(CONV_WIDTH, 3, width).transpose(1, 0, 2)
    y_b = _gdn(proj, colf, rowf, conv_w3, row(gdn_norm), b, s)

    h1 = _merge(y_a, y_b, proj, x2, w_proj_a.astype(BF16), w_proj_b.astype(BF16),
                w_out.astype(BF16), row(ln_post_mix), min(1024, t))
    out = _ffn_ple(h1, p_i.reshape(t, -1), row(ln_pre_ffn), w_ffn_gate.astype(BF16),
                   w_ffn_up.astype(BF16), w_ffn_down.astype(BF16), row(ln_post_ffn),
                   w_ple.astype(BF16), row(ln_ple), w_ple_gate.astype(BF16), min(512, t))
    return out.reshape(b, s, d)


def kernel(x, p, ln_pre_mix, w_in, conv_w, a_log, dt_bias, gdn_norm, w_proj_a, w_proj_b, w_out,
           ln_post_mix, ln_pre_ffn, w_ffn_gate, w_ffn_up, w_ffn_down, ln_post_ffn, w_ple, ln_ple,
           w_ple_gate):
    h = x
    for i in range(p.shape[0]):
        h = _layer(h, p[i], ln_pre_mix[i], w_in[i], conv_w[i], a_log[i], dt_bias[i], gdn_norm[i],
                   w_proj_a[i], w_proj_b[i], w_out[i], ln_post_mix[i], ln_pre_ffn[i],
                   w_ffn_gate[i], w_ffn_up[i], w_ffn_down[i], ln_post_ffn[i], w_ple[i],
                   ln_ple[i], w_ple_gate[i])
    return h
```

```python
import itertools

import jax
import jax.numpy as jnp
from jax import lax
from jax.experimental import pallas as pl
from jax.experimental.pallas import tpu as pltpu

F32 = jnp.float32
BF16 = jnp.bfloat16

HEAD_DIM = 128
N_HEADS = 8
INPROJ_CHUNK = 256
MOBA_BLOCK = 256
MOBA_TOPK = 3
MOBA_HEADS = 4
MOBA_PAST_STEP = 4
GDN_CHUNK = 64
GDN_ROWS = 256
CONV_WIDTH = 4
EPS = 1e-6
ROPE_THETA = 10000.0
NEG_BIG = -1e30
LOG2_E = 1.4426950408889634
VMEM_LIMIT = 56 * 1024 * 1024

_NT = (((1,), (1,)), ((), ()))
_TN = (((0,), (0,)), ((), ()))


def _dot(a, b):
    return jnp.dot(a, b, preferred_element_type=F32)


def _dot_nt(a, b):
    return lax.dot_general(a, b, _NT, preferred_element_type=F32)


def _dot_tn(a, b):
    return lax.dot_general(a, b, _TN, preferred_element_type=F32)


def _rms(x, w):
    return x * lax.rsqrt(jnp.mean(x * x, axis=-1, keepdims=True) + EPS) * w


def _sigmoid(x):
    return 1.0 / (1.0 + jnp.exp(-x))


def _silu(x):
    return x * _sigmoid(x)


def _softplus(x):
    return jnp.maximum(x, 0.0) + jnp.log1p(jnp.exp(-jnp.abs(x)))


def _inproj_kernel(x_ref, ln_ref, w_ref, wvt_ref, ws_ref, wst_ref, cos_ref, sin_ref,
                   alog_ref, dt_ref, o_ref, vt_ref, colf_ref, rowf_ref, u_ref):
    j = pl.program_id(1)

    @pl.when(j == 0)
    def _():
        u = _rms(x_ref[...], ln_ref[...]).astype(BF16)
        u_ref[...] = u
        ab = _dot(u, ws_ref[...])
        abt = _dot_nt(wst_ref[...], u)
        for r0 in range(0, u.shape[0], GDN_ROWS):
            colf, rowf = _gdn_gates(ab[r0:r0 + GDN_ROWS, :], abt[:, r0:r0 + GDN_ROWS],
                                    alog_ref[...], dt_ref[...])
            colf_ref[r0:r0 + GDN_ROWS, :] = colf
            rowf_ref[:, r0:r0 + GDN_ROWS] = rowf
        for c in range(vt_ref.shape[0] // INPROJ_CHUNK):
            rows = slice(c * INPROJ_CHUNK, (c + 1) * INPROJ_CHUNK)
            vt_ref[rows, :] = _dot_nt(wvt_ref[rows, :], u).astype(BF16)

    n_chunks = o_ref.shape[1] // INPROJ_CHUNK
    chunk = lambda c: slice(c * INPROJ_CHUNK, (c + 1) * INPROJ_CHUNK)

    @pl.when(j == 0)
    def _():
        u = u_ref[...]
        cos_k, sin_k = cos_ref[...], sin_ref[...]
        qscale = LOG2_E * HEAD_DIM ** -0.5
        cos_q, sin_q = cos_k * qscale, sin_k * qscale
        for c in range(n_chunks):
            cosf, sinf = (cos_q, sin_q) if c < n_chunks // 2 else (cos_k, sin_k)
            acc = _dot(u, w_ref[:, chunk(c)])
            for h in range(INPROJ_CHUNK // HEAD_DIM):
                seg = acc[:, h * HEAD_DIM:(h + 1) * HEAD_DIM]
                rot = seg * cosf + pltpu.roll(seg, HEAD_DIM // 2, axis=1) * sinf
                lo = c * INPROJ_CHUNK + h * HEAD_DIM
                o_ref[:, lo:lo + HEAD_DIM] = rot.astype(BF16)

    @pl.when(j >= 1)
    def _():
        u = u_ref[...]
        for c in range(n_chunks):
            o_ref[:, chunk(c)] = _dot(u, w_ref[:, chunk(c)]).astype(BF16)


def _inproj(x2, ln, w_main, w_vt, w_small, w_small_t, cosf, sinf, alog_col, dt_col, seq, tm):
    t, d = x2.shape
    n_main = w_main.shape[1]
    n_small = w_small.shape[1]
    n_rows = w_small_t.shape[0]
    width = N_HEADS * HEAD_DIM
    tn = 2 * width
    n_seq_tiles = seq // tm
    assert tm % GDN_ROWS == 0 and n_main % tn == 0
    return pl.pallas_call(
        _inproj_kernel,
        grid=(t // tm, n_main // tn),
        in_specs=[
            pl.BlockSpec((tm, d), lambda i, j: (i, 0)),
            pl.BlockSpec((1, d), lambda i, j: (0, 0)),
            pl.BlockSpec((d, tn), lambda i, j: (0, j)),
            pl.BlockSpec((width, d), lambda i, j: (0, 0)),
            pl.BlockSpec((d, n_small), lambda i, j: (0, 0)),
            pl.BlockSpec((n_rows, d), lambda i, j: (0, 0)),
            pl.BlockSpec((tm, HEAD_DIM), lambda i, j: (i % n_seq_tiles, 0)),
            pl.BlockSpec((tm, HEAD_DIM), lambda i, j: (i % n_seq_tiles, 0)),
            pl.BlockSpec((n_rows, 1), lambda i, j: (0, 0)),
            pl.BlockSpec((n_rows, 1), lambda i, j: (0, 0)),
        ],
        out_specs=[
            pl.BlockSpec((tm, tn), lambda i, j: (i, j)),
            pl.BlockSpec((width, tm), lambda i, j: (0, i)),
            pl.BlockSpec((tm, n_small), lambda i, j: (i, 0)),
            pl.BlockSpec((n_rows, tm), lambda i, j: (0, i)),
        ],
        out_shape=[
            jax.ShapeDtypeStruct((t, n_main), BF16),
            jax.ShapeDtypeStruct((width, t), BF16),
            jax.ShapeDtypeStruct((t, n_small), F32),
            jax.ShapeDtypeStruct((n_rows, t), F32),
        ],
        scratch_shapes=[pltpu.VMEM((tm, d), BF16)],
        compiler_params=pltpu.CompilerParams(
            dimension_semantics=("arbitrary", "arbitrary"), vmem_limit_bytes=VMEM_LIMIT),
        name="inproj",
    )(x2, ln, w_main, w_vt, w_small, w_small_t, cosf, sinf, alog_col, dt_col)


def _moba_kernel(q_ref, kd_ref, vtd_ref, k_ref, vt_ref, o_ref, kmh_ref, kml_ref, s_ref):
    i = pl.program_id(2)
    blk = MOBA_BLOCK
    n_blocks = k_ref.shape[0] // blk
    heads = q_ref.shape[1] // HEAD_DIM
    hsl = lambda g: slice(g * HEAD_DIM, (g + 1) * HEAD_DIM)

    @pl.when(i == 0)
    def _():
        for g in range(heads):
            km = jnp.mean(k_ref[:, hsl(g)].astype(F32).reshape(n_blocks, blk, HEAD_DIM), axis=1)
            hi = km.astype(BF16)
            kmh_ref[g] = hi
            kml_ref[g] = (km - hi.astype(F32)).astype(BF16)

    def attend(n_past):
        sub = lax.broadcasted_iota(jnp.int32, (n_blocks, blk), 0)
        sub_f = sub.astype(F32)
        past = sub < i
        krow = lax.broadcasted_iota(jnp.int32, (blk, blk), 0)
        qcol = lax.broadcasted_iota(jnp.int32, (blk, blk), 1)
        causal = krow <= qcol
        nk = n_past * blk
        halves = ((0, nk // 2), (nk // 2, nk)) if n_past else ()

        def scores(g):
            q = q_ref[:, hsl(g)]
            s_d = jnp.where(causal, _dot_nt(kd_ref[:, hsl(g)], q), -jnp.inf)
            m = jnp.max(s_d, axis=0, keepdims=True)
            if n_past:
                gate = _dot_nt(kmh_ref[g], q) + _dot_nt(kml_ref[g], q)
                gsel = jnp.where(past, gate, -jnp.inf)
                sel = jnp.zeros((n_blocks, blk), jnp.bool_)
                for _ in range(MOBA_TOPK):
                    mx = jnp.max(gsel, axis=0, keepdims=True)
                    idx = jnp.min(jnp.where(gsel == mx, sub_f, float(n_blocks)),
                                  axis=0, keepdims=True)
                    hit = sub_f == idx
                    sel = jnp.logical_or(sel, hit)
                    gsel = jnp.where(hit, -jnp.inf, gsel)
                bias = jnp.where(jnp.logical_and(sel, past), 0.0, NEG_BIG)
                for lo, hi in halves:
                    s_ref[g,lo:hi, :] = _dot_nt(k_ref[lo:hi, hsl(g)], q)
                for c in range(n_past):
                    m_c = jnp.max(s_ref[g,c * blk:(c + 1) * blk, :], axis=0, keepdims=True)
                    m = jnp.maximum(m, m_c + bias[c:c + 1, :])
            else:
                bias = None
            return s_d, m, bias

        def weigh(g, s_d, m, bias):
            p_d = jnp.exp2(s_d - m)
            l = jnp.sum(p_d, axis=0, keepdims=True)
            acc = _dot(vtd_ref[hsl(g), :], p_d.astype(BF16))
            for lo, hi in halves:
                p = jnp.concatenate(
                    [jnp.exp2(s_ref[g,c * blk:(c + 1) * blk, :] + (bias[c:c + 1, :] - m))
                     for c in range(lo // blk, hi // blk)], axis=0)
                l = l + jnp.sum(p, axis=0, keepdims=True)
                acc = acc + _dot(vt_ref[hsl(g), lo:hi], p.astype(BF16))
            o_ref[:, hsl(g)] = (acc * (1.0 / l)).T.astype(BF16)

        pending = scores(0)
        for g in range(heads):
            upcoming = scores(g + 1) if g + 1 < heads else None
            weigh(g, *pending)
            pending = upcoming

    step = MOBA_PAST_STEP
    for n_past in range(0, n_blocks + step, step):
        n_past = min(n_past, n_blocks)
        lo = n_past - step + 1 if n_past else 0
        pl.when(jnp.logical_and(i >= lo, i <= n_past))(lambda n=n_past: attend(n))
        if n_past == n_blocks:
            break


def _moba(proj, vt, batch, seq):
    t = proj.shape[0]
    nb = seq // MOBA_BLOCK
    g = MOBA_HEADS
    gw = g * HEAD_DIM
    k0 = N_HEADS // g
    return pl.pallas_call(
        _moba_kernel,
        grid=(batch, N_HEADS // g, nb),
        in_specs=[
            pl.BlockSpec((MOBA_BLOCK, gw), lambda b, h, i: (b * nb + i, h)),
            pl.BlockSpec((MOBA_BLOCK, gw), lambda b, h, i: (b * nb + i, k0 + h)),
            pl.BlockSpec((gw, MOBA_BLOCK), lambda b, h, i: (h, b * nb + i)),
            pl.BlockSpec((seq, gw), lambda b, h, i: (b, k0 + h)),
            pl.BlockSpec((gw, seq), lambda b, h, i: (h, b)),
        ],
        out_specs=pl.BlockSpec((MOBA_BLOCK, gw), lambda b, h, i: (b * nb + i, h)),
        out_shape=jax.ShapeDtypeStruct((t, N_HEADS * HEAD_DIM), BF16),
        scratch_shapes=[pltpu.VMEM((g, nb, HEAD_DIM), BF16),
                        pltpu.VMEM((g, nb, HEAD_DIM), BF16),
                        pltpu.VMEM((g, seq, MOBA_BLOCK), F32)],
        compiler_params=pltpu.CompilerParams(
            dimension_semantics=("arbitrary", "arbitrary", "arbitrary"),
            vmem_limit_bytes=VMEM_LIMIT),
        name="moba",
    )(proj, proj, vt, proj, vt)


def _chunk_masks(n):
    row = lax.broadcasted_iota(jnp.int32, (n, n), 0)
    col = lax.broadcasted_iota(jnp.int32, (n, n), 1)
    shift = GDN_CHUNK.bit_length() - 1
    same = (row >> shift) == (col >> shift)
    return row, col, same


def _split3(x):
    hi = x.astype(BF16)
    rest = x - hi.astype(F32)
    mid = rest.astype(BF16)
    lo = (rest - mid.astype(F32)).astype(BF16)
    return hi, mid, lo


def _gdn_gates(ab, abt, alog_col, dt_col):
    r = ab.shape[0]
    nh = N_HEADS
    row, col, same = _chunk_masks(r)
    triu = jnp.where(jnp.logical_and(same, row <= col), 1.0, 0.0).astype(BF16)
    ones = jnp.where(same, 1.0, 0.0).astype(BF16)

    srow = lax.broadcasted_iota(jnp.int32, abt.shape, 0)
    gt = -jnp.exp(alog_col) * _softplus(abt + dt_col)
    gt = jnp.where(srow >= nh, gt, 0.0)
    pieces = _split3(gt)
    gc_t = sum(_dot(p, triu) for p in pieces)
    gl_t = sum(_dot(p, ones) for p in pieces)

    stack = jnp.concatenate(
        [gc_t[nh:2 * nh, :], gl_t[nh:2 * nh, :], jnp.zeros((HEAD_DIM - 2 * nh, r), F32)], axis=0)
    shifted = pltpu.roll(stack.T, nh, axis=1)
    lane = lax.broadcasted_iota(jnp.int32, ab.shape, 1)
    return jnp.where(lane < nh, _sigmoid(ab), shifted), gc_t


def _gdn_kernel(q_ref, k_ref, v_ref, z_ref, colf_ref, rowf_ref, cw_ref, gn_ref,
                o_ref, xe_ref, state_ref, cp_ref, at_ref, x_ref, qe_ref, kd_ref, colp_ref):
    sblk = pl.program_id(1)
    r = q_ref.shape[0]
    heads = q_ref.shape[1] // HEAD_DIM
    n_chunks = r // GDN_CHUNK
    n_levels = (GDN_CHUNK - 1).bit_length()
    pad = 8
    hsl = lambda g: slice(g * HEAD_DIM, (g + 1) * HEAD_DIM)
    handover = (cp_ref, at_ref, x_ref, qe_ref, kd_ref, colp_ref)

    @pl.when(jnp.logical_and(pl.program_id(0) == 0, sblk == 0))
    def _():
        for ref in handover:
            ref[...] = jnp.zeros_like(ref)
        state_ref[...] = jnp.zeros_like(state_ref)

    @pl.when(sblk == 0)
    def _():
        xe_ref[...] = jnp.zeros_like(xe_ref)

    def prepare(slot):
        def conv_silu(c, raw_ref, g):
            x = raw_ref[:, hsl(g)].astype(F32)
            xe = jnp.concatenate([xe_ref[c, :, hsl(g)], x], axis=0)
            w = cw_ref[c][:, hsl(g)]
            y = x * w[CONV_WIDTH - 1:CONV_WIDTH, :]
            for d in range(1, CONV_WIDTH):
                y = y + pltpu.roll(xe, d, axis=0)[pad:, :] * w[CONV_WIDTH - 1 - d:CONV_WIDTH - d, :]
            xe_ref[c, :, hsl(g)] = x[r - pad:r, :]
            return _silu(y)

        row, col, same = _chunk_masks(r)
        causal = jnp.logical_and(same, col <= row)
        strict = jnp.logical_and(same, col < row)
        colf = colf_ref[...]
        colp_ref[slot] = colf
        for g in range(heads):
            q = conv_silu(0, q_ref, g)
            k = conv_silu(1, k_ref, g)
            v = conv_silu(2, v_ref, g)
            q = q * lax.rsqrt(jnp.sum(q * q, axis=-1, keepdims=True) + EPS) * (HEAD_DIM ** -0.5)
            k = k * lax.rsqrt(jnp.sum(k * k, axis=-1, keepdims=True) + EPS)

            beta = colf[:, g:g + 1]
            gc = colf[:, heads + g:heads + g + 1]
            glast = colf[:, 2 * heads + g:2 * heads + g + 1]
            gc_row = rowf_ref[heads + g:heads + g + 1, :]
            decay = jnp.exp(jnp.where(causal, gc - gc_row, -jnp.inf))

            kb = k * beta
            k16 = k.astype(BF16)
            cp_ref[slot, g] = jnp.where(
                strict, -_dot_nt(kb.astype(BF16), k16) * decay, 0.0).astype(BF16)
            at_ref[slot, g] = (_dot_nt(q.astype(BF16), k16) * decay).astype(BF16)
            egc = jnp.exp(gc)
            x_ref[slot, g] = jnp.concatenate([v * beta, kb * egc], axis=1)
            qe_ref[slot, g] = q * egc
            kd_ref[slot, g] = (k * jnp.exp(glast - gc)).astype(BF16)
            yield

    def finish(slot):
        hd_range = range(heads)
        c_pow = [cp_ref[slot, g] for g in hd_range]
        x = [x_ref[slot, g] for g in hd_range]
        for level in range(n_levels):
            for g in hd_range:
                x[g] = x[g] + _dot(c_pow[g], x[g].astype(BF16))
                if level + 1 < n_levels:
                    c_pow[g] = _dot(c_pow[g], c_pow[g]).astype(BF16)
            yield

        p_mat, r_mat, nu_g = [], [], []
        for g in hd_range:
            sol16 = x[g].astype(BF16)
            aw_au = _dot(at_ref[slot, g], sol16)
            p_mat.append((qe_ref[slot, g] - aw_au[:, HEAD_DIM:]).astype(BF16))
            r_mat.append(aw_au[:, :HEAD_DIM])
            kd = kd_ref[slot, g]
            nu_g.append([_dot_tn(kd[c * GDN_CHUNK:(c + 1) * GDN_CHUNK, :],
                                 sol16[c * GDN_CHUNK:(c + 1) * GDN_CHUNK, :])
                         for c in range(n_chunks)])
        yield

        colp = colp_ref[slot]
        states = [state_ref[g] for g in hd_range]
        outs = [[] for _ in hd_range]
        for c in range(n_chunks):
            lo, hi = c * GDN_CHUNK, (c + 1) * GDN_CHUNK
            for g in hd_range:
                s16 = states[g].astype(BF16)
                e_c = jnp.exp(colp[lo:lo + 1, 2 * heads + g:2 * heads + g + 1])
                outs[g].append(_dot(p_mat[g][lo:hi, :], s16) + r_mat[g][lo:hi, :])
                states[g] = (e_c * states[g] - _dot(nu_g[g][c][:, HEAD_DIM:].astype(BF16), s16)
                             + nu_g[g][c][:, :HEAD_DIM])
            yield

        gnorm = gn_ref[...]
        for g in hd_range:
            state_ref[g] = jnp.where(sblk > 0, states[g], 0.0)
            o = jnp.concatenate(outs[g], axis=0)
            z = z_ref[:, hsl(g)].astype(F32)
            o_ref[:, hsl(g)] = (_rms(o, gnorm) * _silu(z)).astype(BF16)

    slot_now = sblk % 2
    for _ in itertools.zip_longest(finish(1 - slot_now), prepare(slot_now)):
        pass


def _gdn(proj, colf, rowf, conv_w3, gdn_norm, batch, seq):
    t = proj.shape[0]
    r = GDN_ROWS
    ns = seq // r
    assert ns % 2 == 0
    g = N_HEADS
    gw = g * HEAD_DIM
    qb0, kb0, vb0, z0 = 2, 3, 4, 5
    prep_blk = lambda b, s: b * ns + jnp.minimum(s, ns - 1)
    fin_blk = lambda b, s: b * ns + jnp.maximum(s - 1, 0)

    def tok(off):
        return pl.BlockSpec((r, gw), lambda b, s: (prep_blk(b, s), off))

    return pl.pallas_call(
        _gdn_kernel,
        grid=(batch, ns + 1),
        in_specs=[
            tok(qb0), tok(kb0), tok(vb0),
            pl.BlockSpec((r, gw), lambda b, s: (fin_blk(b, s), z0)),
            pl.BlockSpec((r, HEAD_DIM), lambda b, s: (prep_blk(b, s), 0)),
            pl.BlockSpec((2 * g, r), lambda b, s: (0, prep_blk(b, s))),
            pl.BlockSpec((3, CONV_WIDTH, gw), lambda b, s: (0, 0, 0)),
            pl.BlockSpec((1, HEAD_DIM), lambda b, s: (0, 0)),
        ],
        out_specs=pl.BlockSpec((r, gw), lambda b, s: (fin_blk(b, s), 0)),
        out_shape=jax.ShapeDtypeStruct((t, gw), BF16),
        scratch_shapes=[pltpu.VMEM((3, 8, gw), F32),
                        pltpu.VMEM((g, HEAD_DIM, HEAD_DIM), F32),
                        pltpu.VMEM((2, g, r, r), BF16),
                        pltpu.VMEM((2, g, r, r), BF16),
                        pltpu.VMEM((2, g, r, 2 * HEAD_DIM), F32),
                        pltpu.VMEM((2, g, r, HEAD_DIM), F32),
                        pltpu.VMEM((2, g, r, HEAD_DIM), BF16),
                        pltpu.VMEM((2, r, HEAD_DIM), F32)],
        compiler_params=pltpu.CompilerParams(
            dimension_semantics=("arbitrary", "arbitrary"),
            vmem_limit_bytes=VMEM_LIMIT),
        name="gdn",
    )(proj, proj, proj, proj, colf, rowf, conv_w3, gdn_norm)


def _merge_kernel(ya_ref, yb_ref, ga_ref, gb_ref, x_ref, wa_ref, wb_ref, wo_ref, ln_ref, o_ref):
    merged = (_sigmoid(ga_ref[...].astype(F32)) * _dot(ya_ref[...], wa_ref[...])
              + _sigmoid(gb_ref[...].astype(F32)) * _dot(yb_ref[...], wb_ref[...]))
    out = _dot(merged.astype(BF16), wo_ref[...])
    o_ref[...] = x_ref[...] + _rms(out, ln_ref[...])


def _merge(ya, yb, proj, x2, wa, wb, wo, ln, tm):
    t, d = x2.shape
    gate0 = 6
    const = lambda i: (0, 0)
    return pl.pallas_call(
        _merge_kernel,
        grid=(t // tm,),
        in_specs=[
            pl.BlockSpec((tm, d), lambda i: (i, 0)),
            pl.BlockSpec((tm, d), lambda i: (i, 0)),
            pl.BlockSpec((tm, d), lambda i: (i, gate0)),
            pl.BlockSpec((tm, d), lambda i: (i, gate0 + 1)),
            pl.BlockSpec((tm, d), lambda i: (i, 0)),
            pl.BlockSpec((d, d), const), pl.BlockSpec((d, d), const), pl.BlockSpec((d, d), const),
            pl.BlockSpec((1, d), const),
        ],
        out_specs=pl.BlockSpec((tm, d), lambda i: (i, 0)),
        out_shape=jax.ShapeDtypeStruct((t, d), F32),
        compiler_params=pltpu.CompilerParams(
            dimension_semantics=("arbitrary",), vmem_limit_bytes=VMEM_LIMIT),
        name="merge",
    )(ya, yb, proj, proj, x2, wa, wb, wo, ln)


def _ffn_ple_kernel(h_ref, p_ref, ln1_ref, wg_ref, wu_ref, wd_ref, ln2_ref,
                    wp_ref, lnp_ref, wpg_ref, o_ref):
    h1 = h_ref[...]
    f = _rms(h1, ln1_ref[...]).astype(BF16)
    act = (_silu(_dot(f, wg_ref[...])) * _dot(f, wu_ref[...])).astype(BF16)
    h2 = h1 + _rms(_dot(act, wd_ref[...]), ln2_ref[...])
    e = _rms(_dot(p_ref[...].astype(BF16), wp_ref[...]), lnp_ref[...])
    o_ref[...] = h2 + _sigmoid(_dot(h2.astype(BF16), wpg_ref[...])) * e


def _ffn_ple(h1, p2, ln1, wg, wu, wd, ln2, wp, lnp, wpg, tm):
    t, d = h1.shape
    dff = wg.shape[1]
    dp = p2.shape[1]
    const = lambda i: (0, 0)
    once = pl.Buffered(1)

    def resident(shape):
        return pl.BlockSpec(shape, const, pipeline_mode=once)

    return pl.pallas_call(
        _ffn_ple_kernel,
        grid=(t // tm,),
        in_specs=[
            pl.BlockSpec((tm, d), lambda i: (i, 0)),
            pl.BlockSpec((tm, dp), lambda i: (i, 0)),
            resident((1, d)), resident((d, dff)), resident((d, dff)), resident((dff, d)),
            resident((1, d)), resident((dp, d)), resident((1, d)), resident((d, d)),
        ],
        out_specs=pl.BlockSpec((tm, d), lambda i: (i, 0)),
        out_shape=jax.ShapeDtypeStruct((t, d), F32),
        compiler_params=pltpu.CompilerParams(
            dimension_semantics=("arbitrary",), vmem_limit_bytes=VMEM_LIMIT),
        name="ffn_ple",
    )(h1, p2, ln1, wg, wu, wd, ln2, wp, lnp, wpg)


def _rope_tables(seq):
    inv = 1.0 / (ROPE_THETA ** (jnp.arange(0, HEAD_DIM, 2, dtype=F32) / HEAD_DIM))
    ang = jnp.arange(seq, dtype=F32)[:, None] * inv[None, :]
    cos, sin = jnp.cos(ang), jnp.sin(ang)
    return jnp.concatenate([cos, cos], axis=1), jnp.concatenate([-sin, sin], axis=1)


def _layer(h, p_i, ln_pre_mix, w_in, conv_w, a_log, dt_bias, gdn_norm, w_proj_a, w_proj_b,
           w_out, ln_post_mix, ln_pre_ffn, w_ffn_gate, w_ffn_up, w_ffn_down, ln_post_ffn,
           w_ple, ln_ple, w_ple_gate):
    b, s, d = h.shape
    t = b * s
    width = N_HEADS * HEAD_DIM
    assert d == width and s % MOBA_BLOCK == 0 and s % GDN_ROWS == 0
    x2 = h.reshape(t, d)
    row = lambda v: v.reshape(1, -1).astype(F32)

    c_small = 7 * width
    c_gate = c_small + 2 * N_HEADS
    w_main = jnp.concatenate(
        [w_in[:, :2 * width], w_in[:, 3 * width:c_small], w_in[:, c_gate:]], axis=1).astype(BF16)
    w_vt = w_in[:, 2 * width:3 * width].T.astype(BF16)
    w_small = jnp.pad(w_in[:, c_small:c_gate], ((0, 0), (0, HEAD_DIM - 2 * N_HEADS))).astype(BF16)
    cosf, sinf = _rope_tables(s)
    tm_in = min(1024, s)
    decay_rows = lambda v: jnp.pad(v.astype(F32), (N_HEADS, 0)).reshape(-1, 1)
    proj, vt, colf, rowf = _inproj(
        x2, row(ln_pre_mix), w_main, w_vt, w_small, w_small[:, :2 * N_HEADS].T, cosf, sinf,
        decay_rows(a_log), decay_rows(dt_bias), s, tm_in)

    y_a = _moba(proj, vt, b, s)

    conv_w3 = conv_w.astype(F32).reshape(CONV_WIDTH, 3, width).transpose(1, 0, 2)
    y_b = _gdn(proj, colf, rowf, conv_w3, row(gdn_norm), b, s)

    h1 = _merge(y_a, y_b, proj, x2, w_proj_a.astype(BF16), w_proj_b.astype(BF16),
                w_out.astype(BF16), row(ln_post_mix), min(1024, t))
    out = _ffn_ple(h1, p_i.reshape(t, -1), row(ln_pre_ffn), w_ffn_gate.astype(BF16),
                   w_ffn_up.astype(BF16), w_ffn_down.astype(BF16), row(ln_post_ffn),
                   w_ple.astype(BF16), row(ln_ple), w_ple_gate.astype(BF16), min(512, t))
    return out.reshape(b, s, d)


def kernel(x, p, ln_pre_mix, w_in, conv_w, a_log, dt_bias, gdn_norm, w_proj_a, w_proj_b, w_out,
           ln_post_mix, ln_pre_ffn, w_ffn_gate, w_ffn_up, w_ffn_down, ln_post_ffn, w_ple, ln_ple,
           w_ple_gate):
    h = x
    for i in range(p.shape[0]):
        h = _layer(h, p[i], ln_pre_mix[i], w_in[i], conv_w[i], a_log[i], dt_bias[i], gdn_norm[i],
                   w_proj_a[i], w_proj_b[i], w_out[i], ln_post_mix[i], ln_pre_ffn[i],
                   w_ffn_gate[i], w_ffn_up[i], w_ffn_down[i], ln_post_ffn[i], w_ple[i],
                   ln_ple[i], w_ple_gate[i])
    return h
```

```python
import itertools

import jax
import jax.numpy as jnp
from jax import lax
from jax.experimental import pallas as pl
from jax.experimental.pallas import tpu as pltpu

F32 = jnp.float32
BF16 = jnp.bfloat16

HEAD_DIM = 128
N_HEADS = 8
INPROJ_CHUNK = 256
MOBA_BLOCK = 256
MOBA_TOPK = 3
MOBA_HEADS = 4
MOBA_PAST_STEP = 2
GDN_CHUNK = 64
GDN_ROWS = 256
CONV_WIDTH = 4
EPS = 1e-6
ROPE_THETA = 10000.0
NEG_BIG = -1e30
LOG2_E = 1.4426950408889634
VMEM_LIMIT = 56 * 1024 * 1024

_NT = (((1,), (1,)), ((), ()))
_TN = (((0,), (0,)), ((), ()))


def _dot(a, b):
    return jnp.dot(a, b, preferred_element_type=F32)


def _dot_nt(a, b):
    return lax.dot_general(a, b, _NT, preferred_element_type=F32)


def _dot_tn(a, b):
    return lax.dot_general(a, b, _TN, preferred_element_type=F32)


def _rms(x, w):
    return x * lax.rsqrt(jnp.mean(x * x, axis=-1, keepdims=True) + EPS) * w


def _sigmoid(x):
    return 1.0 / (1.0 + jnp.exp(-x))


def _silu(x):
    return x * _sigmoid(x)


def _softplus(x):
    return jnp.maximum(x, 0.0) + jnp.log1p(jnp.exp(-jnp.abs(x)))


def _inproj_kernel(x_ref, ln_ref, w_ref, wvt_ref, ws_ref, wst_ref, cos_ref, sin_ref,
                   alog_ref, dt_ref, o_ref, vt_ref, colf_ref, rowf_ref, u_ref):
    j = pl.program_id(1)

    @pl.when(j == 0)
    def _():
        u = _rms(x_ref[...], ln_ref[...]).astype(BF16)
        u_ref[...] = u
        ab = _dot(u, ws_ref[...])
        abt = _dot_nt(wst_ref[...], u)
        for r0 in range(0, u.shape[0], GDN_ROWS):
            colf, rowf = _gdn_gates(ab[r0:r0 + GDN_ROWS, :], abt[:, r0:r0 + GDN_ROWS],
                                    alog_ref[...], dt_ref[...])
            colf_ref[r0:r0 + GDN_ROWS, :] = colf
            rowf_ref[:, r0:r0 + GDN_ROWS] = rowf
        for c in range(vt_ref.shape[0] // INPROJ_CHUNK):
            rows = slice(c * INPROJ_CHUNK, (c + 1) * INPROJ_CHUNK)
            vt_ref[rows, :] = _dot_nt(wvt_ref[rows, :], u).astype(BF16)

    n_chunks = o_ref.shape[1] // INPROJ_CHUNK
    chunk = lambda c: slice(c * INPROJ_CHUNK, (c + 1) * INPROJ_CHUNK)

    @pl.when(j == 0)
    def _():
        u = u_ref[...]
        cos_k, sin_k = cos_ref[...], sin_ref[...]
        qscale = LOG2_E * HEAD_DIM ** -0.5
        cos_q, sin_q = cos_k * qscale, sin_k * qscale
        for c in range(n_chunks):
            cosf, sinf = (cos_q, sin_q) if c < n_chunks // 2 else (cos_k, sin_k)
            acc = _dot(u, w_ref[:, chunk(c)])
            for h in range(INPROJ_CHUNK // HEAD_DIM):
                seg = acc[:, h * HEAD_DIM:(h + 1) * HEAD_DIM]
                rot = seg * cosf + pltpu.roll(seg, HEAD_DIM // 2, axis=1) * sinf
                lo = c * INPROJ_CHUNK + h * HEAD_DIM
                o_ref[:, lo:lo + HEAD_DIM] = rot.astype(BF16)

    @pl.when(j >= 1)
    def _():
        u = u_ref[...]
        for c in range(n_chunks):
            o_ref[:, chunk(c)] = _dot(u, w_ref[:, chunk(c)]).astype(BF16)


def _inproj(x2, ln, w_main, w_vt, w_small, w_small_t, cosf, sinf, alog_col, dt_col, seq, tm):
    t, d = x2.shape
    n_main = w_main.shape[1]
    n_small = w_small.shape[1]
    n_rows = w_small_t.shape[0]
    width = N_HEADS * HEAD_DIM
    tn = 2 * width
    n_seq_tiles = seq // tm
    assert tm % GDN_ROWS == 0 and n_main % tn == 0
    return pl.pallas_call(
        _inproj_kernel,
        grid=(t // tm, n_main // tn),
        in_specs=[
            pl.BlockSpec((tm, d), lambda i, j: (i, 0)),
            pl.BlockSpec((1, d), lambda i, j: (0, 0)),
            pl.BlockSpec((d, tn), lambda i, j: (0, j)),
            pl.BlockSpec((width, d), lambda i, j: (0, 0)),
            pl.BlockSpec((d, n_small), lambda i, j: (0, 0)),
            pl.BlockSpec((n_rows, d), lambda i, j: (0, 0)),
            pl.BlockSpec((tm, HEAD_DIM), lambda i, j: (i % n_seq_tiles, 0)),
            pl.BlockSpec((tm, HEAD_DIM), lambda i, j: (i % n_seq_tiles, 0)),
            pl.BlockSpec((n_rows, 1), lambda i, j: (0, 0)),
            pl.BlockSpec((n_rows, 1), lambda i, j: (0, 0)),
        ],
        out_specs=[
            pl.BlockSpec((tm, tn), lambda i, j: (i, j)),
            pl.BlockSpec((width, tm), lambda i, j: (0, i)),
            pl.BlockSpec((tm, n_small), lambda i, j: (i, 0)),
            pl.BlockSpec((n_rows, tm), lambda i, j: (0, i)),
        ],
        out_shape=[
            jax.ShapeDtypeStruct((t, n_main), BF16),
            jax.ShapeDtypeStruct((width, t), BF16),
            jax.ShapeDtypeStruct((t, n_small), F32),
            jax.ShapeDtypeStruct((n_rows, t), F32),
        ],
        scratch_shapes=[pltpu.VMEM((tm, d), BF16)],
        compiler_params=pltpu.CompilerParams(
            dimension_semantics=("arbitrary", "arbitrary"), vmem_limit_bytes=VMEM_LIMIT),
        name="inproj",
    )(x2, ln, w_main, w_vt, w_small, w_small_t, cosf, sinf, alog_col, dt_col)


def _moba_kernel(q_ref, kd_ref, vtd_ref, k_ref, vt_ref, o_ref, kmh_ref, kml_ref, s_ref):
    i = pl.program_id(2)
    blk = MOBA_BLOCK
    n_blocks = k_ref.shape[0] // blk
    heads = q_ref.shape[1] // HEAD_DIM
    hsl = lambda g: slice(g * HEAD_DIM, (g + 1) * HEAD_DIM)

    @pl.when(i == 0)
    def _():
        for g in range(heads):
            km = jnp.mean(k_ref[:, hsl(g)].astype(F32).reshape(n_blocks, blk, HEAD_DIM), axis=1)
            hi = km.astype(BF16)
            kmh_ref[g] = hi
            kml_ref[g] = (km - hi.astype(F32)).astype(BF16)

    def attend(n_past):
        sub = lax.broadcasted_iota(jnp.int32, (n_blocks, blk), 0)
        sub_f = sub.astype(F32)
        past = sub < i
        krow = lax.broadcasted_iota(jnp.int32, (blk, blk), 0)
        qcol = lax.broadcasted_iota(jnp.int32, (blk, blk), 1)
        causal = krow <= qcol
        nk = n_past * blk
        halves = ((0, nk // 2), (nk // 2, nk)) if n_past else ()

        def scores(g):
            q = q_ref[:, hsl(g)]
            s_d = jnp.where(causal, _dot_nt(kd_ref[:, hsl(g)], q), -jnp.inf)
            m = jnp.max(s_d, axis=0, keepdims=True)
            if n_past:
                gate = _dot_nt(kmh_ref[g], q) + _dot_nt(kml_ref[g], q)
                gsel = jnp.where(past, gate, -jnp.inf)
                sel = jnp.zeros((n_blocks, blk), jnp.bool_)
                for _ in range(MOBA_TOPK):
                    mx = jnp.max(gsel, axis=0, keepdims=True)
                    idx = jnp.min(jnp.where(gsel == mx, sub_f, float(n_blocks)),
                                  axis=0, keepdims=True)
                    hit = sub_f == idx
                    sel = jnp.logical_or(sel, hit)
                    gsel = jnp.where(hit, -jnp.inf, gsel)
                bias = jnp.where(jnp.logical_and(sel, past), 0.0, NEG_BIG)
                for lo, hi in halves:
                    s_ref[g,lo:hi, :] = _dot_nt(k_ref[lo:hi, hsl(g)], q)
                for c in range(n_past):
                    m_c = jnp.max(s_ref[g,c * blk:(c + 1) * blk, :], axis=0, keepdims=True)
                    m = jnp.maximum(m, m_c + bias[c:c + 1, :])
            else:
                bias = None
            return s_d, m, bias

        def weigh(g, s_d, m, bias):
            p_d = jnp.exp2(s_d - m)
            l = jnp.sum(p_d, axis=0, keepdims=True)
            acc = _dot(vtd_ref[hsl(g), :], p_d.astype(BF16))
            for lo, hi in halves:
                p = jnp.concatenate(
                    [jnp.exp2(s_ref[g,c * blk:(c + 1) * blk, :] + (bias[c:c + 1, :] - m))
                     for c in range(lo // blk, hi // blk)], axis=0)
                l = l + jnp.sum(p, axis=0, keepdims=True)
                acc = acc + _dot(vt_ref[hsl(g), lo:hi], p.astype(BF16))
            o_ref[:, hsl(g)] = (acc * (1.0 / l)).T.astype(BF16)

        pending = scores(0)
        for g in range(heads):
            upcoming = scores(g + 1) if g + 1 < heads else None
            weigh(g, *pending)
            pending = upcoming

    step = MOBA_PAST_STEP
    for n_past in range(0, n_blocks + step, step):
        n_past = min(n_past, n_blocks)
        lo = n_past - step + 1 if n_past else 0
        pl.when(jnp.logical_and(i >= lo, i <= n_past))(lambda n=n_past: attend(n))
        if n_past == n_blocks:
            break


def _moba(proj, vt, batch, seq):
    t = proj.shape[0]
    nb = seq // MOBA_BLOCK
    g = MOBA_HEADS
    gw = g * HEAD_DIM
    k0 = N_HEADS // g
    return pl.pallas_call(
        _moba_kernel,
        grid=(batch, N_HEADS // g, nb),
        in_specs=[
            pl.BlockSpec((MOBA_BLOCK, gw), lambda b, h, i: (b * nb + i, h)),
            pl.BlockSpec((MOBA_BLOCK, gw), lambda b, h, i: (b * nb + i, k0 + h)),
            pl.BlockSpec((gw, MOBA_BLOCK), lambda b, h, i: (h, b * nb + i)),
            pl.BlockSpec((seq, gw), lambda b, h, i: (b, k0 + h)),
            pl.BlockSpec((gw, seq), lambda b, h, i: (h, b)),
        ],
        out_specs=pl.BlockSpec((MOBA_BLOCK, gw), lambda b, h, i: (b * nb + i, h)),
        out_shape=jax.ShapeDtypeStruct((t, N_HEADS * HEAD_DIM), BF16),
        scratch_shapes=[pltpu.VMEM((g, nb, HEAD_DIM), BF16),
                        pltpu.VMEM((g, nb, HEAD_DIM), BF16),
                        pltpu.VMEM((g, seq, MOBA_BLOCK), F32)],
        compiler_params=pltpu.CompilerParams(
            dimension_semantics=("arbitrary", "arbitrary", "arbitrary"),
            vmem_limit_bytes=VMEM_LIMIT),
        name="moba",
    )(proj, proj, vt, proj, vt)


def _chunk_masks(n):
    row = lax.broadcasted_iota(jnp.int32, (n, n), 0)
    col = lax.broadcasted_iota(jnp.int32, (n, n), 1)
    shift = GDN_CHUNK.bit_length() - 1
    same = (row >> shift) == (col >> shift)
    return row, col, same


def _split3(x):
    hi = x.astype(BF16)
    rest = x - hi.astype(F32)
    mid = rest.astype(BF16)
    lo = (rest - mid.astype(F32)).astype(BF16)
    return hi, mid, lo


def _gdn_gates(ab, abt, alog_col, dt_col):
    r = ab.shape[0]
    nh = N_HEADS
    row, col, same = _chunk_masks(r)
    triu = jnp.where(jnp.logical_and(same, row <= col), 1.0, 0.0).astype(BF16)
    ones = jnp.where(same, 1.0, 0.0).astype(BF16)

    srow = lax.broadcasted_iota(jnp.int32, abt.shape, 0)
    gt = -jnp.exp(alog_col) * _softplus(abt + dt_col)
    gt = jnp.where(srow >= nh, gt, 0.0)
    pieces = _split3(gt)
    gc_t = sum(_dot(p, triu) for p in pieces)
    gl_t = sum(_dot(p, ones) for p in pieces)

    stack = jnp.concatenate(
        [gc_t[nh:2 * nh, :], gl_t[nh:2 * nh, :], jnp.zeros((HEAD_DIM - 2 * nh, r), F32)], axis=0)
    shifted = pltpu.roll(stack.T, nh, axis=1)
    lane = lax.broadcasted_iota(jnp.int32, ab.shape, 1)
    return jnp.where(lane < nh, _sigmoid(ab), shifted), gc_t


def _gdn_kernel(q_ref, k_ref, v_ref, z_ref, colf_ref, rowf_ref, cw_ref, gn_ref,
                o_ref, xe_ref, state_ref, cp_ref, at_ref, x_ref, qe_ref, kd_ref, colp_ref):
    sblk = pl.program_id(1)
    r = q_ref.shape[0]
    heads = q_ref.shape[1] // HEAD_DIM
    n_chunks = r // GDN_CHUNK
    n_levels = (GDN_CHUNK - 1).bit_length()
    pad = 8
    hsl = lambda g: slice(g * HEAD_DIM, (g + 1) * HEAD_DIM)
    handover = (cp_ref, at_ref, x_ref, qe_ref, kd_ref, colp_ref)

    @pl.when(jnp.logical_and(pl.program_id(0) == 0, sblk == 0))
    def _():
        for ref in handover:
            ref[...] = jnp.zeros_like(ref)
        state_ref[...] = jnp.zeros_like(state_ref)

    @pl.when(sblk == 0)
    def _():
        xe_ref[...] = jnp.zeros_like(xe_ref)

    def prepare(slot):
        def conv_silu(c, raw_ref, g):
            x = raw_ref[:, hsl(g)].astype(F32)
            xe = jnp.concatenate([xe_ref[c, :, hsl(g)], x], axis=0)
            w = cw_ref[c][:, hsl(g)]
            y = x * w[CONV_WIDTH - 1:CONV_WIDTH, :]
            for d in range(1, CONV_WIDTH):
                y = y + pltpu.roll(xe, d, axis=0)[pad:, :] * w[CONV_WIDTH - 1 - d:CONV_WIDTH - d, :]
            xe_ref[c, :, hsl(g)] = x[r - pad:r, :]
            return _silu(y)

        row, col, same = _chunk_masks(r)
        causal = jnp.logical_and(same, col <= row)
        strict = jnp.logical_and(same, col < row)
        colf = colf_ref[...]
        colp_ref[slot] = colf
        for g in range(heads):
            q = conv_silu(0, q_ref, g)
            k = conv_silu(1, k_ref, g)
            v = conv_silu(2, v_ref, g)
            q = q * lax.rsqrt(jnp.sum(q * q, axis=-1, keepdims=True) + EPS) * (HEAD_DIM ** -0.5)
            k = k * lax.rsqrt(jnp.sum(k * k, axis=-1, keepdims=True) + EPS)

            beta = colf[:, g:g + 1]
            gc = colf[:, heads + g:heads + g + 1]
            glast = colf[:, 2 * heads + g:2 * heads + g + 1]
            gc_row = rowf_ref[heads + g:heads + g + 1, :]
            decay = jnp.exp(jnp.where(causal, gc - gc_row, -jnp.inf))

            kb = k * beta
            k16 = k.astype(BF16)
            cp_ref[slot, g] = jnp.where(
                strict, -_dot_nt(kb.astype(BF16), k16) * decay, 0.0).astype(BF16)
            at_ref[slot, g] = (_dot_nt(q.astype(BF16), k16) * decay).astype(BF16)
            egc = jnp.exp(gc)
            x_ref[slot, g] = jnp.concatenate([v * beta, kb * egc], axis=1)
            qe_ref[slot, g] = q * egc
            kd_ref[slot, g] = (k * jnp.exp(glast - gc)).astype(BF16)
            yield

    def finish(slot):
        hd_range = range(heads)
        c_pow = [cp_ref[slot, g] for g in hd_range]
        x = [x_ref[slot, g] for g in hd_range]
        for level in range(n_levels):
            for g in hd_range:
                x[g] = x[g] + _dot(c_pow[g], x[g].astype(BF16))
                if level + 1 < n_levels:
                    c_pow[g] = _dot(c_pow[g], c_pow[g]).astype(BF16)
            yield

        p_mat, r_mat, nu_g = [], [], []
        for g in hd_range:
            sol16 = x[g].astype(BF16)
            aw_au = _dot(at_ref[slot, g], sol16)
            p_mat.append((qe_ref[slot, g] - aw_au[:, HEAD_DIM:]).astype(BF16))
            r_mat.append(aw_au[:, :HEAD_DIM])
            kd = kd_ref[slot, g]
            nu_g.append([_dot_tn(kd[c * GDN_CHUNK:(c + 1) * GDN_CHUNK, :],
                                 sol16[c * GDN_CHUNK:(c + 1) * GDN_CHUNK, :])
                         for c in range(n_chunks)])
        yield

        colp = colp_ref[slot]
        states = [state_ref[g] for g in hd_range]
        outs = [[] for _ in hd_range]
        for c in range(n_chunks):
            lo, hi = c * GDN_CHUNK, (c + 1) * GDN_CHUNK
            for g in hd_range:
                s16 = states[g].astype(BF16)
                e_c = jnp.exp(colp[lo:lo + 1, 2 * heads + g:2 * heads + g + 1])
                outs[g].append(_dot(p_mat[g][lo:hi, :], s16) + r_mat[g][lo:hi, :])
                states[g] = (e_c * states[g] - _dot(nu_g[g][c][:, HEAD_DIM:].astype(BF16), s16)
                             + nu_g[g][c][:, :HEAD_DIM])
            yield

        gnorm = gn_ref[...]
        for g in hd_range:
            state_ref[g] = jnp.where(sblk > 0, states[g], 0.0)
            o = jnp.concatenate(outs[g], axis=0)
            z = z_ref[:, hsl(g)].astype(F32)
            o_ref[:, hsl(g)] = (_rms(o, gnorm) * _silu(z)).astype(BF16)

    slot_now = sblk % 2
    for _ in itertools.zip_longest(finish(1 - slot_now), prepare(slot_now)):
        pass


def _gdn(proj, colf, rowf, conv_w3, gdn_norm, batch, seq):
    t = proj.shape[0]
    r = GDN_ROWS
    ns = seq // r
    assert ns % 2 == 0
    g = N_HEADS
    gw = g * HEAD_DIM
    qb0, kb0, vb0, z0 = 2, 3, 4, 5
    prep_blk = lambda b, s: b * ns + jnp.minimum(s, ns - 1)
    fin_blk = lambda b, s: b * ns + jnp.maximum(s - 1, 0)

    def tok(off):
        return pl.BlockSpec((r, gw), lambda b, s: (prep_blk(b, s), off))

    return pl.pallas_call(
        _gdn_kernel,
        grid=(batch, ns + 1),
        in_specs=[
            tok(qb0), tok(kb0), tok(vb0),
            pl.BlockSpec((r, gw), lambda b, s: (fin_blk(b, s), z0)),
            pl.BlockSpec((r, HEAD_DIM), lambda b, s: (prep_blk(b, s), 0)),
            pl.BlockSpec((2 * g, r), lambda b, s: (0, prep_blk(b, s))),
            pl.BlockSpec((3, CONV_WIDTH, gw), lambda b, s: (0, 0, 0)),
            pl.BlockSpec((1, HEAD_DIM), lambda b, s: (0, 0)),
        ],
        out_specs=pl.BlockSpec((r, gw), lambda b, s: (fin_blk(b, s), 0)),
        out_shape=jax.ShapeDtypeStruct((t, gw), BF16),
        scratch_shapes=[pltpu.VMEM((3, 8, gw), F32),
                        pltpu.VMEM((g, HEAD_DIM, HEAD_DIM), F32),
                        pltpu.VMEM((2, g, r, r), BF16),
                        pltpu.VMEM((2, g, r, r), BF16),
                        pltpu.VMEM((2, g, r, 2 * HEAD_DIM), F32),
                        pltpu.VMEM((2, g, r, HEAD_DIM), F32),
                        pltpu.VMEM((2, g, r, HEAD_DIM), BF16),
                        pltpu.VMEM((2, r, HEAD_DIM), F32)],
        compiler_params=pltpu.CompilerParams(
            dimension_semantics=("arbitrary", "arbitrary"),
            vmem_limit_bytes=VMEM_LIMIT),
        name="gdn",
    )(proj, proj, proj, proj, colf, rowf, conv_w3, gdn_norm)


def _merge_kernel(ya_ref, yb_ref, ga_ref, gb_ref, x_ref, wa_ref, wb_ref, wo_ref, ln_ref, o_ref):
    merged = (_sigmoid(ga_ref[...].astype(F32)) * _dot(ya_ref[...], wa_ref[...])
              + _sigmoid(gb_ref[...].astype(F32)) * _dot(yb_ref[...], wb_ref[...]))
    out = _dot(merged.astype(BF16), wo_ref[...])
    o_ref[...] = x_ref[...] + _rms(out, ln_ref[...])


def _merge(ya, yb, proj, x2, wa, wb, wo, ln, tm):
    t, d = x2.shape
    gate0 = 6
    const = lambda i: (0, 0)
    return pl.pallas_call(
        _merge_kernel,
        grid=(t // tm,),
        in_specs=[
            pl.BlockSpec((tm, d), lambda i: (i, 0)),
            pl.BlockSpec((tm, d), lambda i: (i, 0)),
            pl.BlockSpec((tm, d), lambda i: (i, gate0)),
            pl.BlockSpec((tm, d), lambda i: (i, gate0 + 1)),
            pl.BlockSpec((tm, d), lambda i: (i, 0)),
            pl.BlockSpec((d, d), const), pl.BlockSpec((d, d), const), pl.BlockSpec((d, d), const),
            pl.BlockSpec((1, d), const),
        ],
        out_specs=pl.BlockSpec((tm, d), lambda i: (i, 0)),
        out_shape=jax.ShapeDtypeStruct((t, d), F32),
        compiler_params=pltpu.CompilerParams(
            dimension_semantics=("arbitrary",), vmem_limit_bytes=VMEM_LIMIT),
        name="merge",
    )(ya, yb, proj, proj, x2, wa, wb, wo, ln)


def _ffn_ple_kernel(h_ref, p_ref, ln1_ref, wg_ref, wu_ref, wd_ref, ln2_ref,
                    wp_ref, lnp_ref, wpg_ref, o_ref):
    h1 = h_ref[...]
    f = _rms(h1, ln1_ref[...]).astype(BF16)
    act = (_silu(_dot(f, wg_ref[...])) * _dot(f, wu_ref[...])).astype(BF16)
    h2 = h1 + _rms(_dot(act, wd_ref[...]), ln2_ref[...])
    e = _rms(_dot(p_ref[...].astype(BF16), wp_ref[...]), lnp_ref[...])
    o_ref[...] = h2 + _sigmoid(_dot(h2.astype(BF16), wpg_ref[...])) * e


def _ffn_ple(h1, p2, ln1, wg, wu, wd, ln2, wp, lnp, wpg, tm):
    t, d = h1.shape
    dff = wg.shape[1]
    dp = p2.shape[1]
    const = lambda i: (0, 0)
    once = pl.Buffered(1)

    def resident(shape):
        return pl.BlockSpec(shape, const, pipeline_mode=once)

    return pl.pallas_call(
        _ffn_ple_kernel,
        grid=(t // tm,),
        in_specs=[
            pl.BlockSpec((tm, d), lambda i: (i, 0)),
            pl.BlockSpec((tm, dp), lambda i: (i, 0)),
            resident((1, d)), resident((d, dff)), resident((d, dff)), resident((dff, d)),
            resident((1, d)), resident((dp, d)), resident((1, d)), resident((d, d)),
        ],
        out_specs=pl.BlockSpec((tm, d), lambda i: (i, 0)),
        out_shape=jax.ShapeDtypeStruct((t, d), F32),
        compiler_params=pltpu.CompilerParams(
            dimension_semantics=("arbitrary",), vmem_limit_bytes=VMEM_LIMIT),
        name="ffn_ple",
    )(h1, p2, ln1, wg, wu, wd, ln2, wp, lnp, wpg)


def _rope_tables(seq):
    inv = 1.0 / (ROPE_THETA ** (jnp.arange(0, HEAD_DIM, 2, dtype=F32) / HEAD_DIM))
    ang = jnp.arange(seq, dtype=F32)[:, None] * inv[None, :]
    cos, sin = jnp.cos(ang), jnp.sin(ang)
    return jnp.concatenate([cos, cos], axis=1), jnp.concatenate([-sin, sin], axis=1)


def _layer(h, p_i, ln_pre_mix, w_in, conv_w, a_log, dt_bias, gdn_norm, w_proj_a, w_proj_b,
           w_out, ln_post_mix, ln_pre_ffn, w_ffn_gate, w_ffn_up, w_ffn_down, ln_post_ffn,
           w_ple, ln_ple, w_ple_gate):
    b, s, d = h.shape
    t = b * s
    width = N_HEADS * HEAD_DIM
    assert d == width and s % MOBA_BLOCK == 0 and s % GDN_ROWS == 0
    x2 = h.reshape(t, d)
    row = lambda v: v.reshape(1, -1).astype(F32)

    c_small = 7 * width
    c_gate = c_small + 2 * N_HEADS
    w_main = jnp.concatenate(
        [w_in[:, :2 * width], w_in[:, 3 * width:c_small], w_in[:, c_gate:]], axis=1).astype(BF16)
    w_vt = w_in[:, 2 * width:3 * width].T.astype(BF16)
    w_small = jnp.pad(w_in[:, c_small:c_gate], ((0, 0), (0, HEAD_DIM - 2 * N_HEADS))).astype(BF16)
    cosf, sinf = _rope_tables(s)
    tm_in = min(1024, s)
    decay_rows = lambda v: jnp.pad(v.astype(F32), (N_HEADS, 0)).reshape(-1, 1)
    proj, vt, colf, rowf = _inproj(
        x2, row(ln_pre_mix), w_main, w_vt, w_small, w_small[:, :2 * N_HEADS].T, cosf, sinf,
        decay_rows(a_log), decay_rows(dt_bias), s, tm_in)

    y_a = _moba(proj, vt, b, s)

    conv_w3 = conv_w.astype(F32).reshape(CONV_WIDTH, 3, width).transpose(1, 0, 2)
    y_b = _gdn(proj, colf, rowf, conv_w3, row(gdn_norm), b, s)

    h1 = _merge(y_a, y_b, proj, x2, w_proj_a.astype(BF16), w_proj_b.astype(BF16),
                w_out.astype(BF16), row(ln_post_mix), min(1024, t))
    out = _ffn_ple(h1, p_i.reshape(t, -1), row(ln_pre_ffn), w_ffn_gate.astype(BF16),
                   w_ffn_up.astype(BF16), w_ffn_down.astype(BF16), row(ln_post_ffn),
                   w_ple.astype(BF16), row(ln_ple), w_ple_gate.astype(BF16), min(512, t))
    return out.reshape(b, s, d)


def kernel(x, p, ln_pre_mix, w_in, conv_w, a_log, dt_bias, gdn_norm, w_proj_a, w_proj_b, w_out,
           ln_post_mix, ln_pre_ffn, w_ffn_gate, w_ffn_up, w_ffn_down, ln_post_ffn, w_ple, ln_ple,
           w_ple_gate):
    h = x
    for i in range(p.shape[0]):
        h = _layer(h, p[i], ln_pre_mix[i], w_in[i], conv_w[i], a_log[i], dt_bias[i], gdn_norm[i],
                   w_proj_a[i], w_proj_b[i], w_out[i], ln_post_mix[i], ln_pre_ffn[i],
                   w_ffn_gate[i], w_ffn_up[i], w_ffn_down[i], ln_post_ffn[i], w_ple[i],
                   ln_ple[i], w_ple_gate[i])
    return h
```
